```python
import jax
import jax.numpy as jnp
from jax import lax
import numpy as np

D_MODEL = 4096
BATCH = 1
SEQ = 16384
DEPTH = 2

HEAD_DIM = 128
Q_CHUNK = 32
NORM_EPS = 1e-6

MOBA_HEADS = 16
MOBA_BLOCK = 256
MOBA_TOPK = 3
MLSTM_HEADS = 4
MLSTM_QK_DIM = 256
MLSTM_V_DIM = 512
MLSTM_CHUNK = 64
MLSTM_GATE_CAP = 15.0
NSA_HEADS = 16
NSA_KV_HEADS = 4
NSA_CMP_BLOCK = 32
NSA_CMP_STRIDE = 16
NSA_SLC_BLOCK = 64
NSA_SLC_TOPN = 16
NSA_WINDOW = 512
DIL_PATTERNS = ((128, 1), (512, 4), (2048, 16))
DIL_HEADS = 8
N_EXPERTS = 16
N_EXPERT_GROUPS = 4
EXPERTS_PER_GROUP = N_EXPERTS // N_EXPERT_GROUPS
MOE_TOPK = 2
D_FF_EXPERT = 1024

MOBA_W = MOBA_HEADS * HEAD_DIM
MLSTM_QK_W = MLSTM_HEADS * MLSTM_QK_DIM
MLSTM_V_W = MLSTM_HEADS * MLSTM_V_DIM
EVEN_IN_SIZES = (MOBA_W, MOBA_W, MOBA_W, MLSTM_QK_W, MLSTM_QK_W, MLSTM_V_W, MLSTM_V_W, MLSTM_HEADS, MLSTM_HEADS)
EVEN_IN_W = sum(EVEN_IN_SIZES)
EVEN_OUT_W = MOBA_W + MLSTM_V_W
NSA_Q_W = NSA_HEADS * HEAD_DIM
NSA_KV_W = NSA_KV_HEADS * HEAD_DIM
DIL_W = len(DIL_PATTERNS) * DIL_HEADS * HEAD_DIM
ODD_IN_SIZES = (NSA_Q_W,) + (NSA_KV_W,) * 6 + (3 * NSA_HEADS,) + (DIL_W,) * 3
ODD_IN_W = sum(ODD_IN_SIZES)
ODD_OUT_W = NSA_Q_W + DIL_HEADS * HEAD_DIM
N_EVEN = (DEPTH + 1) // 2
N_ODD = DEPTH // 2

kernel_name = "hybrid_moba_mlstm_nsa_dilated_moe"


def rms_norm(x, gain):
    xf = x.astype(jnp.float32)
    xf = xf * lax.rsqrt(jnp.mean(xf * xf, axis=-1, keepdims=True) + NORM_EPS)
    return (xf * gain.astype(jnp.float32)).astype(x.dtype)


def split_cols(z, sizes):
    return jnp.split(z, np.cumsum(sizes)[:-1].tolist(), axis=-1)


def pad_seq(a, mult, front=0):
    back = (-a.shape[1]) % mult
    if back == 0 and front == 0:
        return a
    pad = [(0, 0)] * a.ndim
    pad[1] = (front, back)
    return jnp.pad(a, pad)


def to_chunks(a, size):
    B, T = a.shape[0], a.shape[1]
    return jnp.swapaxes(a.reshape((B, T // size, size) + a.shape[2:]), 0, 1)


def from_chunks(a):
    return jnp.swapaxes(a, 0, 1).reshape((a.shape[1], a.shape[0] * a.shape[2]) + a.shape[3:])


def masked_softmax(s, mask):
    s = jnp.where(mask, s, -jnp.inf)
    m = jnp.max(s, axis=-1, keepdims=True)
    m = jnp.where(jnp.isfinite(m), m, 0.0)
    p = jnp.exp(s - m)
    den = jnp.sum(p, axis=-1, keepdims=True)
    return p / jnp.where(den > 0, den, 1.0), m + jnp.log(den)


def moba_attention(q, k, v):
    B, T, H, Dh = q.shape
    nb = T // MOBA_BLOCK
    n_top = min(MOBA_TOPK, nb)
    scale = Dh ** -0.5
    k5 = k.reshape(B, nb, MOBA_BLOCK, H, Dh)
    k_mean = jnp.mean(k5.astype(jnp.float32), axis=2)
    k_blk = k5.transpose(0, 3, 1, 2, 4)
    v_blk = v.reshape(B, nb, MOBA_BLOCK, H, Dh).transpose(0, 3, 1, 2, 4)
    b_ix = jnp.arange(B)[:, None, None, None]
    h_ix = jnp.arange(H)[None, None, :, None]

    def step(args):
        q_c, start = args
        pos = start + jnp.arange(Q_CHUNK)
        cur = start // MOBA_BLOCK
        gate = jnp.einsum('bqhd,bnhd->bqhn', q_c.astype(jnp.float32), k_mean)
        gate = jnp.where(jnp.arange(nb) < cur, gate, -jnp.inf)
        _, idx = lax.top_k(gate, n_top)
        valid = idx < cur
        k_sel = k_blk[b_ix, h_ix, idx]
        v_sel = v_blk[b_ix, h_ix, idx]
        s_sel = jnp.einsum('bqhd,bqhksd->bqhks', q_c, k_sel, preferred_element_type=jnp.float32) * scale
        s_sel = jnp.where(valid[..., None], s_sel, -jnp.inf)
        blk_start = cur * MOBA_BLOCK
        k_own = lax.dynamic_slice_in_dim(k, blk_start, MOBA_BLOCK, axis=1)
        v_own = lax.dynamic_slice_in_dim(v, blk_start, MOBA_BLOCK, axis=1)
        s_own = jnp.einsum('bqhd,bshd->bqhs', q_c, k_own, preferred_element_type=jnp.float32) * scale
        own_mask = (blk_start + jnp.arange(MOBA_BLOCK))[None, :] <= pos[:, None]
        s_own = jnp.where(own_mask[None, :, None, :], s_own, -jnp.inf)
        m = jnp.maximum(jnp.max(s_own, axis=-1), jnp.max(s_sel, axis=(-2, -1)))
        p_own = jnp.exp(s_own - m[..., None])
        p_sel = jnp.exp(s_sel - m[..., None, None])
        den = jnp.sum(p_own, axis=-1) + jnp.sum(p_sel, axis=(-2, -1))
        out = (jnp.einsum('bqhs,bshd->bqhd', p_own, v_own.astype(jnp.float32))
               + jnp.einsum('bqhks,bqhksd->bqhd', p_sel, v_sel.astype(jnp.float32)))
        return (out / den[..., None]).astype(q.dtype)

    starts = jnp.arange(T // Q_CHUNK, dtype=jnp.int32) * Q_CHUNK
    return from_chunks(lax.map(step, (to_chunks(q, Q_CHUNK), starts)))


def mlstm_chunkwise(q, k, v, i_pre, f_pre):
    B, T, H, Dk = q.shape
    Dv = v.shape[-1]
    L = MLSTM_CHUNK
    qf = q.astype(jnp.float32) * Dk ** -0.5
    log_f = jax.nn.log_sigmoid(f_pre.astype(jnp.float32))
    log_i = i_pre.astype(jnp.float32)
    causal = jnp.asarray(np.tril(np.ones((L, L), dtype=bool)))

    def step(carry, xs):
        C, n, m = carry
        qc, kc, vc, li, lf = xs
        b = jnp.cumsum(lf, axis=1)
        logD = b[:, :, None, :] - b[:, None, :, :] + li[:, None, :, :]
        logD = jnp.where(causal[None, :, :, None], logD, -jnp.inf)
        log_inter = b + m[:, None, :]
        m_s = jnp.maximum(log_inter, jnp.max(logD, axis=2))
        dmat = jnp.exp(logD - m_s[:, :, None, :])
        inter_w = jnp.exp(log_inter - m_s)
        s = jnp.einsum('bshd,brhd->bsrh', qc, kc) * dmat
        num = (jnp.einsum('bsrh,brhv->bshv', s, vc)
               + inter_w[..., None] * jnp.einsum('bhvd,bshd->bshv', C, qc))
        den = jnp.sum(s, axis=2) + inter_w * jnp.einsum('bhd,bshd->bsh', n, qc)
        h = num / jnp.maximum(jnp.abs(den), jnp.exp(-m_s))[..., None]
        b_end = b[:, -1]
        log_w = b_end[:, None, :] - b + li
        m_new = jnp.maximum(b_end + m, jnp.max(log_w, axis=1))
        w = jnp.exp(log_w - m_new[:, None, :])
        decay = jnp.exp(b_end + m - m_new)
        C = decay[..., None, None] * C + jnp.einsum('brh,brhv,brhd->bhvd', w, vc, kc)
        n = decay[..., None] * n + jnp.einsum('brh,brhd->bhd', w, kc)
        return (C, n, m_new), h

    init = (jnp.zeros((B, H, Dv, Dk), jnp.float32), jnp.zeros((B, H, Dk), jnp.float32),
            jnp.zeros((B, H), jnp.float32))
    xs = (to_chunks(qf, L), to_chunks(k.astype(jnp.float32), L), to_chunks(v.astype(jnp.float32), L),
          to_chunks(log_i, L), to_chunks(log_f, L))
    _, hs = lax.scan(step, init, xs)
    return from_chunks(hs)


def moba_mlstm_mixer(h, w_in, w_out, qn_g, kn_g, i_b, f_b, out_g):
    B, T, _ = h.shape
    z = pad_seq(jnp.einsum('btd,dn->btn', h, w_in), MOBA_BLOCK)
    Tp = z.shape[1]
    qa, ka, va, qb, kb, vb, ob, ib, fb = split_cols(z, EVEN_IN_SIZES)
    qa = rms_norm(qa.reshape(B, Tp, MOBA_HEADS, HEAD_DIM), qn_g)
    ka = rms_norm(ka.reshape(B, Tp, MOBA_HEADS, HEAD_DIM), kn_g)
    va = va.reshape(B, Tp, MOBA_HEADS, HEAD_DIM)
    o_a = moba_attention(qa, ka, va).reshape(B, Tp, MOBA_W)
    cap = MLSTM_GATE_CAP
    i_pre = cap * jnp.tanh((ib.astype(jnp.float32) + i_b.astype(jnp.float32)) / cap)
    f_pre = cap * jnp.tanh((fb.astype(jnp.float32) + f_b.astype(jnp.float32)) / cap)
    hb = mlstm_chunkwise(qb.reshape(B, Tp, MLSTM_HEADS, MLSTM_QK_DIM),
                         kb.reshape(B, Tp, MLSTM_HEADS, MLSTM_QK_DIM),
                         vb.reshape(B, Tp, MLSTM_HEADS, MLSTM_V_DIM), i_pre, f_pre)
    hb = rms_norm(hb, out_g.reshape(MLSTM_HEADS, MLSTM_V_DIM)).reshape(B, Tp, MLSTM_V_W)
    o_b = (hb * jax.nn.sigmoid(ob.astype(jnp.float32))).astype(h.dtype)
    o = jnp.concatenate([o_a, o_b], axis=-1)[:, :T]
    return jnp.einsum('btn,nd->btd', o, w_out)


def nsa_compress(kv, pe, w1, b1, w2):
    B, T, G, Dh = kv.shape
    n_cmp = (T - NSA_CMP_BLOCK) // NSA_CMP_STRIDE + 1
    idx = np.arange(n_cmp)[:, None] * NSA_CMP_STRIDE + np.arange(NSA_CMP_BLOCK)[None, :]
    blocks = kv[:, idx] + pe[:, None, :].astype(kv.dtype)
    blocks = jnp.swapaxes(blocks, 2, 3).reshape(B, n_cmp, G, NSA_CMP_BLOCK * Dh)
    return jax.nn.gelu(blocks @ w1 + b1) @ w2


def nsa_dilated_mixer(h, w_in, w_out, nsa_qn_g, nsa_kn_g, pe_k, phik_w1, phik_b1, phik_w2,
                      pe_v, phiv_w1, phiv_b1, phiv_w2, dil_qn_g, dil_kn_g):
    B, T, _ = h.shape
    hd, G = HEAD_DIM, NSA_KV_HEADS
    R = NSA_HEADS // G
    NP, HG = len(DIL_PATTERNS), DIL_HEADS
    z = pad_seq(jnp.einsum('btd,dn->btn', h, w_in), NSA_SLC_BLOCK)
    Tp = z.shape[1]
    q, kc, vc, ks, vs, kw, vw, g, dq, dk, dv = split_cols(z, ODD_IN_SIZES)
    heads = lambda a, n: a.reshape(B, Tp, n, hd)
    q = rms_norm(heads(q, NSA_HEADS), nsa_qn_g)
    ks = rms_norm(heads(ks, G), nsa_kn_g)
    kw = rms_norm(heads(kw, G), nsa_kn_g)
    vs, vw = heads(vs, G), heads(vw, G)
    gates = jax.nn.sigmoid(g.astype(jnp.float32)).reshape(B, Tp, NSA_HEADS, 3)
    dq = rms_norm(dq.reshape(B, Tp, NP, HG, hd), dil_qn_g)
    dk = rms_norm(dk.reshape(B, Tp, NP, HG, hd), dil_kn_g)
    dv = dv.reshape(B, Tp, NP, HG, hd)
    dk_g = [dk[:, :, gi] for gi in range(NP)]
    dv_g = [dv[:, :, gi] for gi in range(NP)]
    k_cmp = rms_norm(nsa_compress(heads(kc, G), pe_k, phik_w1, phik_b1, phik_w2), nsa_kn_g)
    v_cmp = nsa_compress(heads(vc, G), pe_v, phiv_w1, phiv_b1, phiv_w2)
    n_cmp = k_cmp.shape[1]
    cmp_end = np.arange(n_cmp) * NSA_CMP_STRIDE + NSA_CMP_BLOCK - 1
    n_slc = Tp // NSA_SLC_BLOCK
    n_top = min(NSA_SLC_TOPN, n_slc)
    ks_blk = ks.reshape(B, n_slc, NSA_SLC_BLOCK, G, hd).transpose(0, 3, 1, 2, 4)
    vs_blk = vs.reshape(B, n_slc, NSA_SLC_BLOCK, G, hd).transpose(0, 3, 1, 2, 4)
    ratio = NSA_SLC_BLOCK // NSA_CMP_STRIDE
    slc_w = np.array([0.5] + [1.0] * (ratio - 1), np.float32)
    kw_p = pad_seq(kw, 1, front=NSA_WINDOW)
    vw_p = pad_seq(vw, 1, front=NSA_WINDOW)
    b_ix = jnp.arange(B)[:, None, None, None]
    g_ix = jnp.arange(G)[None, None, :, None]
    scale = hd ** -0.5
    blk = jnp.arange(n_slc)

    def step(args):
        q_c, g_c, dq_c, start = args
        pos = start + jnp.arange(Q_CHUNK)
        qg = q_c.reshape(B, Q_CHUNK, G, R, hd)
        s = jnp.einsum('bqgrd,bngd->bqgrn', qg, k_cmp, preferred_element_type=jnp.float32) * scale
        cmask = jnp.asarray(cmp_end)[None, :] <= pos[:, None]
        p_cmp, _ = masked_softmax(s, cmask[None, :, None, None, :])
        o_cmp = jnp.einsum('bqgrn,bngd->bqgrd', p_cmp, v_cmp.astype(jnp.float32))
        imp = jnp.pad(jnp.sum(p_cmp, axis=3), [(0, 0)] * 3 + [(1, 1)])
        p_slc = (imp[..., :ratio * n_slc].reshape(B, Q_CHUNK, G, n_slc, ratio) @ slc_w
                 + 0.5 * imp[..., ratio::ratio])
        cur = pos // NSA_SLC_BLOCK
        forced = (blk[None, :] == 0) | (blk[None, :] == cur[:, None]) | (blk[None, :] == cur[:, None] - 1)
        future = blk[None, :] > cur[:, None]
        score = jnp.where(future[None, :, None, :], -jnp.inf,
                          jnp.where(forced[None, :, None, :], jnp.inf, p_slc))
        _, idx = lax.top_k(score, n_top)
        k_sel = ks_blk[b_ix, g_ix, idx]
        v_sel = vs_blk[b_ix, g_ix, idx].reshape(B, Q_CHUNK, G, n_top * NSA_SLC_BLOCK, hd)
        s = jnp.einsum('bqgrd,bqgnsd->bqgrns', qg, k_sel, preferred_element_type=jnp.float32) * scale
        s = s.reshape(B, Q_CHUNK, G, R, n_top * NSA_SLC_BLOCK)
        key_pos = idx[..., None] * NSA_SLC_BLOCK + jnp.arange(NSA_SLC_BLOCK)
        smask = (key_pos <= pos[None, :, None, None, None]).reshape(B, Q_CHUNK, G, 1, n_top * NSA_SLC_BLOCK)
        p, _ = masked_softmax(s, smask)
        o_slc = jnp.einsum('bqgrk,bqgkd->bqgrd', p, v_sel.astype(jnp.float32))
        kwin = lax.dynamic_slice_in_dim(kw_p, start, NSA_WINDOW + Q_CHUNK, axis=1)
        vwin = lax.dynamic_slice_in_dim(vw_p, start, NSA_WINDOW + Q_CHUNK, axis=1)
        wpos = start - NSA_WINDOW + jnp.arange(NSA_WINDOW + Q_CHUNK)
        wmask = ((wpos[None, :] <= pos[:, None]) & (wpos[None, :] > pos[:, None] - NSA_WINDOW)
                 & (wpos[None, :] >= 0))
        s = jnp.einsum('bqgrd,bkgd->bqgrk', qg, kwin, preferred_element_type=jnp.float32) * scale
        p, _ = masked_softmax(s, wmask[None, :, None, None, :])
        o_win = jnp.einsum('bqgrk,bkgd->bqgrd', p, vwin.astype(jnp.float32))
        gc = g_c.reshape(B, Q_CHUNK, G, R, 3)
        o_nsa = (gc[..., 0:1] * o_cmp + gc[..., 1:2] * o_slc + gc[..., 2:3] * o_win)
        o_nsa = o_nsa.reshape(B, Q_CHUNK, NSA_Q_W)
        outs, lses = [], []
        for gi, (win, dil) in enumerate(DIL_PATTERNS):
            offs = np.arange(win // dil + 1) * dil
            kpos = pos[:, None] - jnp.asarray(offs)[None, :]
            kidx = jnp.maximum(kpos, 0)
            kg = jnp.take(dk_g[gi], kidx, axis=1)
            vg = jnp.take(dv_g[gi], kidx, axis=1)
            s = jnp.einsum('bqhd,bqjhd->bqhj', dq_c[:, :, gi], kg, preferred_element_type=jnp.float32) * scale
            p, lse = masked_softmax(s, (kpos >= 0)[None, :, None, :])
            outs.append(jnp.einsum('bqhj,bqjhd->bqhd', p, vg.astype(jnp.float32)))
            lses.append(lse[..., 0])
        alpha = jax.nn.softmax(jnp.stack(lses, axis=2), axis=2)
        o_dil = jnp.sum(alpha[..., None] * jnp.stack(outs, axis=2), axis=2).reshape(B, Q_CHUNK, HG * hd)
        return jnp.concatenate([o_nsa, o_dil], axis=-1).astype(h.dtype)

    starts = jnp.arange(Tp // Q_CHUNK, dtype=jnp.int32) * Q_CHUNK
    xs = (to_chunks(q, Q_CHUNK), to_chunks(gates, Q_CHUNK), to_chunks(dq, Q_CHUNK), starts)
    o = from_chunks(lax.map(step, xs))[:, :T]
    return jnp.einsum('btn,nd->btd', o, w_out)


def moe_ffn(h, router_w, router_b, w_gate, w_up, w_down):
    B, T, _ = h.shape
    s = jax.nn.sigmoid(jnp.einsum('btd,de->bte', h, router_w, preferred_element_type=jnp.float32))
    sel = s + router_b.astype(jnp.float32)
    sel_g = sel.reshape(B, T, N_EXPERT_GROUPS, EXPERTS_PER_GROUP)
    grp = jnp.argmax(jnp.sum(lax.top_k(sel_g, 2)[0], axis=-1), axis=-1)
    in_grp = jnp.take_along_axis(sel_g, grp[..., None, None], axis=2)[..., 0, :]
    _, loc = lax.top_k(in_grp, MOE_TOPK)
    eidx = grp[..., None] * EXPERTS_PER_GROUP + loc
    s_sel = jnp.take_along_axis(s, eidx, axis=-1)
    wts = s_sel / jnp.sum(s_sel, axis=-1, keepdims=True)
    combine = jnp.sum(jax.nn.one_hot(eidx, N_EXPERTS, dtype=jnp.float32) * wts[..., None], axis=-2)
    combine = combine.astype(h.dtype)
    y = jnp.zeros_like(h)
    for e in range(N_EXPERTS):
        act = jax.nn.silu(h @ w_gate[e]) * (h @ w_up[e])
        y = y + combine[..., e:e + 1] * (act @ w_down[e])
    return y


def setup_inputs(seed: int = 0) -> dict:
    key = jax.random.key(seed)
    keys = list(jax.random.split(key, 48))
    f32 = jnp.float32
    D, hd, F, E = D_MODEL, HEAD_DIM, D_FF_EXPERT, N_EXPERTS

    def nrm(shape, scale):
        return jax.random.normal(keys.pop(), shape, f32) * scale

    def gain(shape):
        return 1.0 + 0.02 * jax.random.normal(keys.pop(), shape, f32)

    return {
        "x": nrm((BATCH, SEQ, D), 1.0),
        "c": nrm((BATCH, D), 1.0),
        "ada_w": nrm((DEPTH, D, 6 * D), 0.5 * D ** -0.5),
        "ada_b": nrm((DEPTH, 6 * D), 0.02),
        "norm_mix_g": gain((DEPTH, D)),
        "norm_ffn_g": gain((DEPTH, D)),
        "ev_w_in": nrm((N_EVEN, D, EVEN_IN_W), D ** -0.5),
        "ev_w_out": nrm((N_EVEN, EVEN_OUT_W, D), EVEN_OUT_W ** -0.5),
        "moba_qn_g": gain((N_EVEN, hd)),
        "moba_kn_g": gain((N_EVEN, hd)),
        "mlstm_i_b": nrm((N_EVEN, MLSTM_HEADS), 0.5),
        "mlstm_f_b": 3.0 + 3.0 * jax.random.uniform(keys.pop(), (N_EVEN, MLSTM_HEADS), f32),
        "mlstm_out_g": gain((N_EVEN, MLSTM_V_W)),
        "od_w_in": nrm((N_ODD, D, ODD_IN_W), D ** -0.5),
        "od_w_out": nrm((N_ODD, ODD_OUT_W, D), ODD_OUT_W ** -0.5),
        "nsa_qn_g": gain((N_ODD, hd)),
        "nsa_kn_g": gain((N_ODD, hd)),
        "nsa_pe_k": nrm((N_ODD, NSA_CMP_BLOCK, hd), 0.1),
        "nsa_phik_w1": nrm((N_ODD, NSA_CMP_BLOCK * hd, hd), (NSA_CMP_BLOCK * hd) ** -0.5),
        "nsa_phik_b1": nrm((N_ODD, hd), 0.02),
        "nsa_phik_w2": nrm((N_ODD, hd, hd), hd ** -0.5),
        "nsa_pe_v": nrm((N_ODD, NSA_CMP_BLOCK, hd), 0.1),
        "nsa_phiv_w1": nrm((N_ODD, NSA_CMP_BLOCK * hd, hd), (NSA_CMP_BLOCK * hd) ** -0.5),
        "nsa_phiv_b1": nrm((N_ODD, hd), 0.02),
        "nsa_phiv_w2": nrm((N_ODD, hd, hd), hd ** -0.5),
        "dil_qn_g": gain((N_ODD, hd)),
        "dil_kn_g": gain((N_ODD, hd)),
        "router_w": nrm((D, E), D ** -0.5),
        "router_b": nrm((E,), 0.01),
        "moe_w_gate": nrm((DEPTH, E, D, F), D ** -0.5),
        "moe_w_up": nrm((DEPTH, E, D, F), D ** -0.5),
        "moe_w_down": nrm((DEPTH, E, F, D), F ** -0.5),
    }


def reference(x, c, ada_w, ada_b, norm_mix_g, norm_ffn_g, ev_w_in, ev_w_out, moba_qn_g, moba_kn_g,
              mlstm_i_b, mlstm_f_b, mlstm_out_g, od_w_in, od_w_out, nsa_qn_g, nsa_kn_g, nsa_pe_k,
              nsa_phik_w1, nsa_phik_b1, nsa_phik_w2, nsa_pe_v, nsa_phiv_w1, nsa_phiv_b1, nsa_phiv_w2,
              dil_qn_g, dil_kn_g, router_w, router_b, moe_w_gate, moe_w_up, moe_w_down):
    c_act = jax.nn.silu(c)
    for layer in range(DEPTH):
        mod = c_act @ ada_w[layer] + ada_b[layer]
        sh_m, sc_m, g_m, sh_f, sc_f, g_f = [m[:, None, :] for m in jnp.split(mod, 6, axis=-1)]
        h_mix = rms_norm(x, norm_mix_g[layer]) * (1 + sc_m) + sh_m
        j = layer // 2
        if layer % 2 == 0:
            y = moba_mlstm_mixer(h_mix, ev_w_in[j], ev_w_out[j], moba_qn_g[j], moba_kn_g[j],
                                 mlstm_i_b[j], mlstm_f_b[j], mlstm_out_g[j])
        else:
            y = nsa_dilated_mixer(h_mix, od_w_in[j], od_w_out[j], nsa_qn_g[j], nsa_kn_g[j],
                                  nsa_pe_k[j], nsa_phik_w1[j], nsa_phik_b1[j], nsa_phik_w2[j],
                                  nsa_pe_v[j], nsa_phiv_w1[j], nsa_phiv_b1[j], nsa_phiv_w2[j],
                                  dil_qn_g[j], dil_kn_g[j])
        x = x + g_m * y
        h_ffn = rms_norm(x, norm_ffn_g[layer]) * (1 + sc_f) + sh_f
        x = x + g_f * moe_ffn(h_ffn, router_w, router_b, moe_w_gate[layer], moe_w_up[layer],
                              moe_w_down[layer])
    return x
```

```python
import functools

import numpy as np
import jax
import jax.numpy as jnp
from jax import lax
from jax.experimental import pallas as pl
from jax.experimental.pallas import tpu as pltpu

F32 = jnp.float32
BF16 = jnp.bfloat16
I32 = jnp.int32

NORM_EPS = 1e-6
HEAD_DIM = 128
LANES = 128
VMEM_LIMIT_BYTES = 56 * 1024 * 1024
NEG = -1e30

MOBA_HEADS = 16
MOBA_BLOCK = 256
MOBA_TOPK = 3
MLSTM_HEADS = 4
MLSTM_QK_DIM = 256
MLSTM_V_DIM = 512
MLSTM_GATE_CAP = 15.0
NSA_HEADS = 16
NSA_KV_HEADS = 4
NSA_CMP_BLOCK = 32
NSA_CMP_STRIDE = 16
NSA_SLC_BLOCK = 64
NSA_SLC_TOPN = 16
NSA_WINDOW = 512
DIL_PATTERNS = ((128, 1), (512, 4), (2048, 16))
DIL_HEADS = 8
N_EXPERTS = 16
N_EXPERT_GROUPS = 4
EXPERTS_PER_GROUP = N_EXPERTS // N_EXPERT_GROUPS
D_FF_EXPERT = 1024

_NT = (((1,), (1,)), ((), ()))
_TN = (((0,), (0,)), ((), ()))


def _params(*sem):
    return pltpu.CompilerParams(dimension_semantics=sem, vmem_limit_bytes=VMEM_LIMIT_BYTES)


def _split_bf16(a):
    hi = a.astype(BF16)
    lo = (a - hi.astype(F32)).astype(BF16)
    return hi, lo


def _dot_nt_f32(a, b):
    a_hi, a_lo = _split_bf16(a)
    b_hi, b_lo = _split_bf16(b)
    d = functools.partial(lax.dot_general, dimension_numbers=_NT, preferred_element_type=F32)
    return d(a_hi, b_hi) + (d(a_hi, b_lo) + d(a_lo, b_hi))


def _rms(x, gain):
    return x * lax.rsqrt(jnp.mean(x * x, axis=-1, keepdims=True) + NORM_EPS) * gain


def _adaln_kernel(c_ref, w_ref, b_ref, o_ref):
    c = c_ref[...]
    ca = c * jax.nn.sigmoid(c)
    o_ref[...] = jnp.sum(w_ref[...] * ca, axis=0, keepdims=True) + b_ref[...]


def adaln(c, ada_w, ada_b):
    depth, d, n = ada_w.shape
    tn = 512
    out = pl.pallas_call(
        _adaln_kernel,
        grid=(depth, n // tn),
        in_specs=[pl.BlockSpec((d, 1), lambda l, j: (0, 0)),
                  pl.BlockSpec((None, d, tn), lambda l, j: (l, 0, j)),
                  pl.BlockSpec((None, 1, tn), lambda l, j: (l, 0, j))],
        out_specs=pl.BlockSpec((None, 1, tn), lambda l, j: (l, 0, j)),
        out_shape=jax.ShapeDtypeStruct((depth, 1, n), F32),
        compiler_params=_params("parallel", "parallel"),
        name="adaln",
    )(c.reshape(d, 1), ada_w, ada_b.reshape(depth, 1, n))
    return out.reshape(depth, n)


def _norm_matmul_kernel(x_ref, a_ref, b_ref, w_ref, o_ref, h_ref):
    @pl.when(pl.program_id(1) == 0)
    def _():
        h_ref[...] = (_rms(x_ref[...], a_ref[...]) + b_ref[...]).astype(BF16)

    o_ref[...] = jnp.dot(h_ref[...], w_ref[...], preferred_element_type=F32).astype(o_ref.dtype)


def norm_matmul(x, a, b, w, *, tm=512, tn=512, out_dtype=F32):
    t, d = x.shape
    n = w.shape[1]
    tn = min(tn, n)
    return pl.pallas_call(
        _norm_matmul_kernel,
        grid=(t // tm, n // tn),
        in_specs=[pl.BlockSpec((tm, d), lambda i, j: (i, 0)),
                  pl.BlockSpec((1, d), lambda i, j: (0, 0)),
                  pl.BlockSpec((1, d), lambda i, j: (0, 0)),
                  pl.BlockSpec((d, tn), lambda i, j: (0, j))],
        out_specs=pl.BlockSpec((tm, tn), lambda i, j: (i, j)),
        out_shape=jax.ShapeDtypeStruct((t, n), out_dtype),
        scratch_shapes=[pltpu.VMEM((tm, d), BF16)],
        compiler_params=_params("parallel", "arbitrary"),
        name="norm_matmul",
    )(x, a, b, w)


def _matmul_residual_kernel(a_ref, w_ref, x_ref, g_ref, o_ref):
    y = jnp.dot(a_ref[...], w_ref[...], preferred_element_type=F32)
    o_ref[...] = x_ref[...] + g_ref[...] * y


def matmul_residual(a, w, x, g, *, tm=512, tn=512):
    t, k = a.shape
    d = w.shape[1]
    tn = min(tn, d)
    return pl.pallas_call(
        _matmul_residual_kernel,
        grid=(t // tm, d // tn),
        in_specs=[pl.BlockSpec((tm, k), lambda i, j: (i, 0)),
                  pl.BlockSpec((k, tn), lambda i, j: (0, j)),
                  pl.BlockSpec((tm, tn), lambda i, j: (i, j)),
                  pl.BlockSpec((1, tn), lambda i, j: (0, j))],
        out_specs=pl.BlockSpec((tm, tn), lambda i, j: (i, j)),
        out_shape=jax.ShapeDtypeStruct((t, d), F32),
        compiler_params=_params("parallel", "parallel"),
        name="matmul_residual",
    )(a, w, x, g)


def _colprep_kernel(z_ref, g_ref, *o_refs, norm, scale, transpose, mean_rows):
    y = z_ref[...]
    if norm:
        y = _rms(y, g_ref[...])
    if mean_rows:
        tm = y.shape[0]
        o_refs[1][...] = jnp.mean(y.reshape(tm // mean_rows, mean_rows, y.shape[1]), axis=1)
    if scale != 1.0:
        y = y * scale
    o_refs[0][...] = (y.T if transpose else y).astype(o_refs[0].dtype)


def colprep(z, col0, nheads, gain=None, *, scale=1.0, transpose=False, mean_rows=0, tm=2048,
            out_dtype=BF16):
    t = z.shape[0]
    tm = min(tm, t)
    c0 = col0 // HEAD_DIM
    norm = gain is not None
    g = (gain if norm else jnp.ones((HEAD_DIM,), F32)).reshape(1, HEAD_DIM)
    w = nheads * HEAD_DIM
    if transpose:
        out_shape = [jax.ShapeDtypeStruct((w, t), out_dtype)]
        out_specs = [pl.BlockSpec((HEAD_DIM, tm), lambda i, h: (h, i))]
    else:
        out_shape = [jax.ShapeDtypeStruct((t, w), out_dtype)]
        out_specs = [pl.BlockSpec((tm, HEAD_DIM), lambda i, h: (i, h))]
    if mean_rows:
        out_shape.append(jax.ShapeDtypeStruct((t // mean_rows, w), F32))
        out_specs.append(pl.BlockSpec((tm // mean_rows, HEAD_DIM), lambda i, h: (i, h)))
    outs = pl.pallas_call(
        functools.partial(_colprep_kernel, norm=norm, scale=scale, transpose=transpose,
                          mean_rows=mean_rows),
        grid=(t // tm, nheads),
        in_specs=[pl.BlockSpec((tm, HEAD_DIM), lambda i, h: (i, c0 + h)),
                  pl.BlockSpec((1, HEAD_DIM), lambda i, h: (0, 0))],
        out_specs=out_specs,
        out_shape=out_shape,
        compiler_params=_params("parallel", "parallel"),
        name="colprep",
    )(z, g)
    return outs if mean_rows else outs[0]


def _moba_kernel(zq_ref, g_ref, kn_ref, vt_ref, km_ref, o_ref, sel_ref, acc_ref, *, tq, nb, topk):
    i = pl.program_id(1)
    qn = _rms(zq_ref[...], g_ref[...])
    qs = (qn * HEAD_DIM ** -0.5).astype(BF16)
    pos = i * tq + lax.broadcasted_iota(I32, (1, tq), 1)
    cur = (pos // MOBA_BLOCK).astype(F32)
    bidx = lax.broadcasted_iota(I32, (nb, tq), 0).astype(F32)

    gate = jnp.where(bidx < cur, _dot_nt_f32(km_ref[...], qn), -jnp.inf)
    sel = jnp.zeros((nb, tq), F32)
    for _ in range(topk):
        mx = jnp.max(gate, axis=0, keepdims=True)
        is_max = (gate == mx) & (gate > -jnp.inf)
        first = jnp.min(jnp.where(is_max, bidx, float(nb)), axis=0, keepdims=True)
        pick = bidx == first
        sel = jnp.where(pick, 1.0, sel)
        gate = jnp.where(pick, -jnp.inf, gate)
    sel_ref[...] = jnp.where(bidx == cur, 1.0, sel)
    acc_ref[...] = jnp.zeros_like(acc_ref)

    def body(j, carry):
        m, l = carry
        off = pl.multiple_of(j * MOBA_BLOCK, MOBA_BLOCK)
        k = kn_ref[pl.ds(off, MOBA_BLOCK), :]
        s = lax.dot_general(k, qs, _NT, preferred_element_type=F32)
        kpos = off + lax.broadcasted_iota(I32, (MOBA_BLOCK, 1), 0)
        mask = (sel_ref[pl.ds(j, 1), :] > 0.5) & (kpos <= pos)
        s = jnp.where(mask, s, NEG)
        m_new = jnp.maximum(m, jnp.max(s, axis=0, keepdims=True))
        alpha = jnp.exp(m - m_new)
        p = jnp.exp(s - m_new)
        l_new = alpha * l + jnp.sum(p, axis=0, keepdims=True)
        vt = vt_ref[:, pl.ds(off, MOBA_BLOCK)]
        acc_ref[...] = alpha * acc_ref[...] + jnp.dot(vt, p.astype(BF16), preferred_element_type=F32)
        return m_new, l_new

    n_blocks = (i + 1) * (tq // MOBA_BLOCK)
    _, l = lax.fori_loop(0, n_blocks, body, (jnp.full((1, tq), NEG, F32), jnp.zeros((1, tq), F32)))
    o_ref[...] = (acc_ref[...] / l).T.astype(o_ref.dtype)


def moba_attention(z, q_col0, q_gain, kn, vt, kmean, *, tq=512):
    t = z.shape[0]
    tq = min(tq, t)
    nb = t // MOBA_BLOCK
    h = MOBA_HEADS
    c0 = q_col0 // HEAD_DIM
    return pl.pallas_call(
        functools.partial(_moba_kernel, tq=tq, nb=nb, topk=min(MOBA_TOPK, nb)),
        grid=(h, t // tq),
        in_specs=[pl.BlockSpec((tq, HEAD_DIM), lambda hh, i: (i, c0 + hh)),
                  pl.BlockSpec((1, HEAD_DIM), lambda hh, i: (0, 0)),
                  pl.BlockSpec((t, HEAD_DIM), lambda hh, i: (0, hh)),
                  pl.BlockSpec((HEAD_DIM, t), lambda hh, i: (hh, 0)),
                  pl.BlockSpec((nb, HEAD_DIM), lambda hh, i: (0, hh))],
        out_specs=pl.BlockSpec((tq, HEAD_DIM), lambda hh, i: (i, hh)),
        out_shape=jax.ShapeDtypeStruct((t, h * HEAD_DIM), BF16),
        scratch_shapes=[pltpu.VMEM((nb, tq), F32), pltpu.VMEM((HEAD_DIM, tq), F32)],
        compiler_params=_params("parallel", "parallel"),
        name="moba",
    )(z, q_gain.reshape(1, HEAD_DIM), kn, vt, kmean)


def _mlstm_kernel(zq_ref, zk_ref, zv_ref, zo_ref, zg_ref, gb_ref, og_ref, o_ref, ct_ref, n_ref, m_ref,
                  *, chunk):
    dk, dv, cap = MLSTM_QK_DIM, MLSTM_V_DIM, MLSTM_GATE_CAP

    @pl.when(pl.program_id(0) == 0)
    def _():
        ct_ref[...] = jnp.zeros_like(ct_ref)
        n_ref[...] = jnp.zeros_like(n_ref)
        m_ref[...] = jnp.zeros_like(m_ref)

    pre = cap * jnp.tanh((zg_ref[...] + gb_ref[...]) / cap)
    log_f = jnp.minimum(pre, 0.0) - jnp.log1p(jnp.exp(-jnp.abs(pre)))
    row = lax.broadcasted_iota(I32, (chunk, chunk), 0)
    col = lax.broadcasted_iota(I32, (chunk, chunk), 1)
    causal = col <= row
    tri = causal.astype(BF16)
    p1 = log_f.astype(BF16)
    r1 = log_f - p1.astype(F32)
    p2 = r1.astype(BF16)
    p3 = (r1 - p2.astype(F32)).astype(BF16)
    mm = functools.partial(jnp.dot, preferred_element_type=F32)
    b = mm(tri, p1) + (mm(tri, p2) + mm(tri, p3))
    pre_t = pre.T
    b_t = b.T

    for h in range(MLSTM_HEADS):
        li_row = pre_t[h:h + 1, :]
        li_col = pre[:, h:h + 1]
        b_row = b_t[MLSTM_HEADS + h:MLSTM_HEADS + h + 1, :]
        b_col = b[:, MLSTM_HEADS + h:MLSTM_HEADS + h + 1]
        m_prev = m_ref[h][:, :1]
        log_d = jnp.where(causal, b_col - b_row + li_row, -jnp.inf)
        log_inter = b_col + m_prev
        m_s = jnp.maximum(log_inter, jnp.max(log_d, axis=1, keepdims=True))
        dmat = jnp.exp(log_d - m_s)
        inter_w = jnp.exp(log_inter - m_s)
        qf = zq_ref[:, h * dk:(h + 1) * dk] * dk ** -0.5
        kf = zk_ref[:, h * dk:(h + 1) * dk]
        q16 = qf.astype(BF16)
        v16 = zv_ref[:, h * dv:(h + 1) * dv].astype(BF16)
        s = lax.dot_general(q16, kf.astype(BF16), _NT, preferred_element_type=F32) * dmat
        ct = ct_ref[h]
        num = mm(s.astype(BF16), v16) + inter_w * mm(q16, ct.astype(BF16))
        den = (jnp.sum(s, axis=1, keepdims=True)
               + inter_w * jnp.sum(qf * n_ref[h], axis=1, keepdims=True))
        hh = num / jnp.maximum(jnp.abs(den), jnp.exp(-m_s))
        hn = _rms(hh, og_ref[:, h * dv:(h + 1) * dv])
        o_ref[:, h * dv:(h + 1) * dv] = (hn * jax.nn.sigmoid(zo_ref[:, h * dv:(h + 1) * dv])).astype(o_ref.dtype)
        b_end = b_col[chunk - 1:chunk, :]
        log_w = b_end - b_col + li_col
        m_new = jnp.maximum(b_end + m_prev, jnp.max(log_w, axis=0, keepdims=True))
        kw = kf * jnp.exp(log_w - m_new)
        decay = jnp.exp(b_end + m_prev - m_new)
        ct_ref[h] = decay * ct + lax.dot_general(kw.astype(BF16), v16, _TN, preferred_element_type=F32)
        n_ref[h] = decay * n_ref[h] + jnp.sum(kw, axis=0, keepdims=True)
        m_ref[h] = jnp.broadcast_to(m_new, (1, LANES))


def mlstm(z, q_col0, k_col0, v_col0, o_col0, zg, gate_bias, out_gain, *, chunk=256):
    t = z.shape[0]
    chunk = min(chunk, t)
    wk, wv = MLSTM_HEADS * MLSTM_QK_DIM, MLSTM_HEADS * MLSTM_V_DIM
    return pl.pallas_call(
        functools.partial(_mlstm_kernel, chunk=chunk),
        grid=(t // chunk,),
        in_specs=[pl.BlockSpec((chunk, wk), lambda c: (c, q_col0 // wk)),
                  pl.BlockSpec((chunk, wk), lambda c: (c, k_col0 // wk)),
                  pl.BlockSpec((chunk, wv), lambda c: (c, v_col0 // wv)),
                  pl.BlockSpec((chunk, wv), lambda c: (c, o_col0 // wv)),
                  pl.BlockSpec((chunk, LANES), lambda c: (c, 0)),
                  pl.BlockSpec((1, LANES), lambda c: (0, 0)),
                  pl.BlockSpec((1, wv), lambda c: (0, 0))],
        out_specs=pl.BlockSpec((chunk, wv), lambda c: (c, 0)),
        out_shape=jax.ShapeDtypeStruct((t, wv), BF16),
        scratch_shapes=[pltpu.VMEM((MLSTM_HEADS, MLSTM_QK_DIM, MLSTM_V_DIM), F32),
                        pltpu.VMEM((MLSTM_HEADS, 1, MLSTM_QK_DIM), F32),
                        pltpu.VMEM((MLSTM_HEADS, 1, LANES), F32)],
        compiler_params=_params("arbitrary"),
        name="mlstm",
    )(z, z, z, z, zg, gate_bias, out_gain.reshape(1, wv))


def _nsa_compress_kernel(a_ref, pe_ref, w1_ref, b1_ref, w2_ref, g_ref, o_ref, *, is_key):
    n, half = a_ref.shape
    w1 = w1_ref[...]
    bias = jnp.sum(w1 * pe_ref[...], axis=0, keepdims=True) + b1_ref[...]
    w1 = w1.astype(BF16)
    a16 = a_ref[...].astype(BF16)
    u = jnp.dot(a16, w1[:half], preferred_element_type=F32)
    v = jnp.dot(a16, w1[half:], preferred_element_type=F32)
    x = u + pltpu.roll(v, n - 1, 0) + bias
    hid = 0.5 * x * (1.0 + jnp.tanh(np.sqrt(2.0 / np.pi) * (x + 0.044715 * (x * x * x))))
    y = jnp.dot(hid.astype(BF16), w2_ref[...].astype(BF16), preferred_element_type=F32)
    if is_key:
        y = _rms(y, g_ref[...])
    o_ref[...] = y.astype(o_ref.dtype)


def nsa_compress(a, pe, w1, b1, w2, gain, *, is_key):
    g, n, half = a.shape
    hd = HEAD_DIM
    out_shape, out_spec = jax.ShapeDtypeStruct((g, n, hd), BF16), pl.BlockSpec((None, n, hd), lambda i: (i, 0, 0))
    full = lambda shape: pl.BlockSpec(shape, lambda i: (0,) * len(shape))
    return pl.pallas_call(
        functools.partial(_nsa_compress_kernel, is_key=is_key),
        grid=(g,),
        in_specs=[pl.BlockSpec((None, n, half), lambda i: (i, 0, 0)),
                  full((2 * half, 1)), full((2 * half, hd)), full((1, hd)), full((hd, hd)), full((1, hd))],
        out_specs=out_spec,
        out_shape=out_shape,
        compiler_params=_params("parallel"),
        name="nsa_compress",
    )(a, pe.reshape(2 * half, 1), w1, b1.reshape(1, hd), w2, gain.reshape(1, hd))


NSA_SLC_KEY_TILE = 256


def _online_softmax_step(s, mask, m, l, acc_ref, slot, vt):
    s = jnp.where(mask, s, NEG)
    m_new = jnp.maximum(m, jnp.max(s, axis=0, keepdims=True))
    alpha = jnp.exp(m - m_new)
    p = jnp.exp(s - m_new)
    l_new = alpha * l + jnp.sum(p, axis=0, keepdims=True)
    acc_ref[slot] = alpha * acc_ref[slot] + jnp.dot(vt, p.astype(BF16), preferred_element_type=F32)
    return m_new, l_new


def _nsa_kernel(zq_ref, qg_ref, gate_ref, kc_ref, vc_ref, mt_ref, ks_ref, vst_ref, kw_ref, vwt_ref,
                o_ref, sel_ref, acc_ref, *, tq, nc, ns, ntop):
    i = pl.program_id(1)
    r_heads = NSA_HEADS // NSA_KV_HEADS
    w = r_heads * tq
    hd = HEAD_DIM
    scale = hd ** -0.5
    qs = jnp.concatenate(
        [(_rms(zq_ref[:, r * hd:(r + 1) * hd], qg_ref[...]) * scale).astype(BF16) for r in range(r_heads)],
        axis=0)
    pos_t = i * tq + lax.broadcasted_iota(I32, (1, tq), 1)
    pos = i * tq + (lax.broadcasted_iota(I32, (1, w), 1) & (tq - 1))

    s = lax.dot_general(kc_ref[...], qs, _NT, preferred_element_type=F32)
    cmp_end = lax.broadcasted_iota(I32, (nc, 1), 0) * NSA_CMP_STRIDE + (NSA_CMP_BLOCK - 1)
    cmask = cmp_end <= pos
    s = jnp.where(cmask, s, NEG)
    p = jnp.where(cmask, jnp.exp(s - jnp.max(s, axis=0, keepdims=True)), 0.0)
    den = jnp.sum(p, axis=0, keepdims=True)
    p = p * (1.0 / jnp.where(den > 0.0, den, 1.0))
    o_cmp = lax.dot_general(vc_ref[...], p.astype(BF16), _TN, preferred_element_type=F32)
    imp = p[:, 0:tq]
    for r in range(1, r_heads):
        imp = imp + p[:, r * tq:(r + 1) * tq]
    imp_hi, imp_lo = _split_bf16(imp)
    p_slc = (jnp.dot(mt_ref[...], imp_hi, preferred_element_type=F32)
             + jnp.dot(mt_ref[...], imp_lo, preferred_element_type=F32))

    blk = lax.broadcasted_iota(I32, (ns, tq), 0).astype(F32)
    cur = (pos_t // NSA_SLC_BLOCK).astype(F32)
    forced = (blk == 0.0) | (blk == cur) | (blk == cur - 1.0)
    score0 = jnp.where(blk > cur, -jnp.inf, jnp.where(forced, jnp.inf, p_slc))

    def pick_one(_, carry):
        score, sel = carry
        mx = jnp.max(score, axis=0, keepdims=True)
        is_max = (score == mx) & (score > -jnp.inf)
        first = jnp.min(jnp.where(is_max, blk, float(ns)), axis=0, keepdims=True)
        pick = blk == first
        return jnp.where(pick, -jnp.inf, score), jnp.where(pick, 1.0, sel)

    _, sel = lax.fori_loop(0, ntop, pick_one, (score0, jnp.zeros((ns, tq), F32)))
    sel_ref[...] = sel
    acc_ref[...] = jnp.zeros_like(acc_ref)
    init = (jnp.full((1, w), NEG, F32), jnp.zeros((1, w), F32))

    tk = NSA_SLC_KEY_TILE
    nblk = tk // NSA_SLC_BLOCK

    def slc_body(j, carry):
        off = pl.multiple_of(j * tk, tk)
        s = lax.dot_general(ks_ref[pl.ds(off, tk), :], qs, _NT, preferred_element_type=F32)
        rows = sel_ref[pl.ds(j * nblk, nblk), :]
        rows = jnp.concatenate([rows] * r_heads, axis=1)
        bmask = jnp.concatenate(
            [jnp.broadcast_to(rows[b:b + 1, :], (NSA_SLC_BLOCK, w)) for b in range(nblk)], axis=0)
        kpos = off + lax.broadcasted_iota(I32, (tk, 1), 0)
        mask = (bmask > 0.5) & (kpos <= pos)
        return _online_softmax_step(s, mask, *carry, acc_ref, 0, vst_ref[:, pl.ds(off, tk)])

    _, l_slc = lax.fori_loop(0, (i * tq + tq - 1) // tk + 1, slc_body, init)

    def win_body(j, carry):
        off = pl.multiple_of(j * tq, tq)
        s = lax.dot_general(kw_ref[pl.ds(off, tq), :], qs, _NT, preferred_element_type=F32)
        kpos = off + lax.broadcasted_iota(I32, (tq, 1), 0)
        mask = (kpos <= pos) & (kpos > pos - NSA_WINDOW)
        return _online_softmax_step(s, mask, *carry, acc_ref, 1, vwt_ref[:, pl.ds(off, tq)])

    _, l_win = lax.fori_loop(jnp.maximum(i - NSA_WINDOW // tq, 0), i + 1, win_body, init)

    gates = jax.nn.sigmoid(gate_ref[...])
    o_slc = acc_ref[0] / l_slc
    o_win = acc_ref[1] / l_win
    for r in range(r_heads):
        cols = slice(r * tq, (r + 1) * tq)
        o = (gates[0, r:r + 1, :] * o_cmp[:, cols] + gates[1, r:r + 1, :] * o_slc[:, cols]
             + gates[2, r:r + 1, :] * o_win[:, cols])
        o_ref[:, r * hd:(r + 1) * hd] = o.T.astype(o_ref.dtype)


def nsa_slc_weights(ns, nc):
    ratio = NSA_SLC_BLOCK // NSA_CMP_STRIDE
    m = np.zeros((ns, nc), np.float32)
    for j in range(ns):
        for d, wgt in [(-1, 0.5)] + [(k, 1.0) for k in range(ratio - 1)] + [(ratio - 1, 0.5)]:
            n = ratio * j + d
            if 0 <= n < nc - 1:
                m[j, n] = wgt
    return m


def nsa_attention(z, q_col0, q_gain, gates_t, kcmp, vcmp, ksn, vs_t, kwn, vw_t, *, tq=128):
    t = z.shape[0]
    g, r_heads, hd = NSA_KV_HEADS, NSA_HEADS // NSA_KV_HEADS, HEAD_DIM
    nc, ns = t // NSA_CMP_STRIDE, t // NSA_SLC_BLOCK
    mt = jnp.asarray(nsa_slc_weights(ns, nc), BF16)
    qw = r_heads * hd
    once = pl.Buffered(1)
    return pl.pallas_call(
        functools.partial(_nsa_kernel, tq=tq, nc=nc, ns=ns, ntop=min(NSA_SLC_TOPN, ns)),
        grid=(g, t // tq),
        in_specs=[pl.BlockSpec((tq, qw), lambda gg, i: (i, q_col0 // qw + gg)),
                  pl.BlockSpec((1, hd), lambda gg, i: (0, 0)),
                  pl.BlockSpec((None, 3, r_heads, tq), lambda gg, i: (gg, 0, 0, i)),
                  pl.BlockSpec((None, nc, hd), lambda gg, i: (gg, 0, 0)),
                  pl.BlockSpec((None, nc, hd), lambda gg, i: (gg, 0, 0)),
                  pl.BlockSpec((ns, nc), lambda gg, i: (0, 0)),
                  pl.BlockSpec((t, hd), lambda gg, i: (0, gg), pipeline_mode=once),
                  pl.BlockSpec((hd, t), lambda gg, i: (gg, 0), pipeline_mode=once),
                  pl.BlockSpec((t, hd), lambda gg, i: (0, gg), pipeline_mode=once),
                  pl.BlockSpec((hd, t), lambda gg, i: (gg, 0), pipeline_mode=once)],
        out_specs=pl.BlockSpec((tq, qw), lambda gg, i: (i, gg)),
        out_shape=jax.ShapeDtypeStruct((t, NSA_HEADS * hd), BF16),
        scratch_shapes=[pltpu.VMEM((ns, tq), F32), pltpu.VMEM((2, hd, r_heads * tq), F32)],
        compiler_params=_params("parallel", "arbitrary"),
        name="nsa",
    )(z, q_gain.reshape(1, hd), gates_t, kcmp, vcmp, mt, ksn, vs_t, kwn, vw_t)


def _dilated_kernel(q_ref, kp_ref, kc_ref, vp_ref, vc_ref, o_ref, lse_ref, *, tq, back):
    i = pl.program_id(1)
    hd = HEAD_DIM
    qpos = i * tq + lax.broadcasted_iota(I32, (tq, 1), 0)
    kpos = (i - 1) * tq + lax.broadcasted_iota(I32, (1, 2 * tq), 1)
    mask = (kpos <= qpos) & (kpos >= qpos - back) & (kpos >= 0)
    for j in range(DIL_HEADS):
        cols = slice(j * hd, (j + 1) * hd)
        k = jnp.concatenate([kp_ref[:, cols], kc_ref[:, cols]], axis=0)
        v = jnp.concatenate([vp_ref[:, cols], vc_ref[:, cols]], axis=0)
        s = lax.dot_general(q_ref[:, cols], k, _NT, preferred_element_type=F32)
        s = jnp.where(mask, s, NEG)
        m = jnp.max(s, axis=1, keepdims=True)
        p = jnp.exp(s - m)
        l = jnp.sum(p, axis=1, keepdims=True)
        o_ref[:, cols] = jnp.dot(p.astype(BF16), v, preferred_element_type=F32) / l
        lse_ref[:, cols] = jnp.broadcast_to(m + jnp.log(l), (tq, hd))


def dilated_group(qd, kd, vd, dil, *, back, tq=128):
    td, wd = qd.shape
    hw = wd // dil
    tq = min(tq, td)
    cur = pl.BlockSpec((tq, hw), lambda r, i: (i, r))
    prev = pl.BlockSpec((tq, hw), lambda r, i: (jnp.maximum(i - 1, 0), r))
    return pl.pallas_call(
        functools.partial(_dilated_kernel, tq=tq, back=back),
        grid=(dil, td // tq),
        in_specs=[cur, prev, cur, prev, cur],
        out_specs=[cur, cur],
        out_shape=[jax.ShapeDtypeStruct((td, wd), F32), jax.ShapeDtypeStruct((td, wd), F32)],
        compiler_params=_params("parallel", "parallel"),
        name="dilated",
    )(qd, kd, kd, vd, vd)


def _dilated_merge_kernel(*refs):
    n = (len(refs) - 1) // 2
    o_refs, lse_refs, out_ref = refs[:n], refs[n:2 * n], refs[2 * n]
    lse = [r[...] for r in lse_refs]
    m = functools.reduce(jnp.maximum, lse)
    e = [jnp.exp(x - m) for x in lse]
    tot = functools.reduce(lambda a, b: a + b, e)
    acc = sum(ei * r[...] for ei, r in zip(e, o_refs))
    out_ref[...] = (acc / tot).astype(out_ref.dtype)


def dilated_merge(outs, lses, *, tm=512):
    t, wd = outs[0].shape
    tm = min(tm, t)
    spec = pl.BlockSpec((tm, wd), lambda i: (i, 0))
    return pl.pallas_call(
        _dilated_merge_kernel,
        grid=(t // tm,),
        in_specs=[spec] * (2 * len(outs)),
        out_specs=spec,
        out_shape=jax.ShapeDtypeStruct((t, wd), BF16),
        compiler_params=_params("parallel"),
        name="dilated_merge",
    )(*outs, *lses)


def _top2_of_4(v):
    best, loc0 = v[0], jnp.zeros_like(v[0])
    for j in range(1, 4):
        better = v[j] > best
        best = jnp.where(better, v[j], best)
        loc0 = jnp.where(better, float(j), loc0)
    best1, loc1 = jnp.full_like(v[0], -jnp.inf), jnp.zeros_like(v[0])
    for j in range(4):
        better = (v[j] > best1) & (loc0 != float(j))
        best1 = jnp.where(better, v[j], best1)
        loc1 = jnp.where(better, float(j), loc1)
    return loc0, loc1


def _router_kernel(x_ref, a_ref, b_ref, wt_ref, rb_ref, h_ref, info_ref, cnt_ref, carry_ref, *, tm):
    ne, ng, eg = N_EXPERTS, N_EXPERT_GROUPS, EXPERTS_PER_GROUP

    @pl.when(pl.program_id(0) == 0)
    def _():
        carry_ref[...] = jnp.zeros_like(carry_ref)

    h = _rms(x_ref[...], a_ref[...]) + b_ref[...]
    h_ref[...] = h
    s = jax.nn.sigmoid(_dot_nt_f32(wt_ref[...], h))
    sel = s + rb_ref[...]
    srow = [s[e:e + 1, :] for e in range(ne)]
    selrow = [sel[e:e + 1, :] for e in range(ne)]
    gscore = []
    for g in range(ng):
        a0, a1, a2, a3 = selrow[eg * g:eg * g + 4]
        hi1, lo1, hi2, lo2 = jnp.maximum(a0, a1), jnp.minimum(a0, a1), jnp.maximum(a2, a3), jnp.minimum(a2, a3)
        gscore.append(jnp.maximum(hi1, hi2) + jnp.maximum(jnp.minimum(hi1, hi2), jnp.maximum(lo1, lo2)))
    best, grp = gscore[0], jnp.zeros_like(gscore[0])
    for g in range(1, ng):
        better = gscore[g] > best
        best = jnp.where(better, gscore[g], best)
        grp = jnp.where(better, float(g), grp)
    pick = lambda rows, j: sum(jnp.where(grp == float(g), rows[eg * g + j], 0.0) for g in range(ng))
    loc0, loc1 = _top2_of_4([pick(selrow, j) for j in range(eg)])
    s_in = [pick(srow, j) for j in range(eg)]
    s0 = sum(jnp.where(loc0 == float(j), s_in[j], 0.0) for j in range(eg))
    s1 = sum(jnp.where(loc1 == float(j), s_in[j], 0.0) for j in range(eg))
    e0 = grp * eg + loc0
    e1 = grp * eg + loc1
    tot = s0 + s1
    eidx = lax.broadcasted_iota(I32, (ne, tm), 0).astype(F32)
    oh0 = eidx == e0
    oh1 = eidx == e1
    onehot = (oh0 | oh1).astype(BF16)
    upper = (lax.broadcasted_iota(I32, (tm, tm), 0) < lax.broadcasted_iota(I32, (tm, tm), 1)).astype(BF16)
    before = jnp.dot(onehot, upper, preferred_element_type=F32) + carry_ref[:, :1]
    r0 = jnp.sum(jnp.where(oh0, before, 0.0), axis=0, keepdims=True)
    r1 = jnp.sum(jnp.where(oh1, before, 0.0), axis=0, keepdims=True)
    carry_ref[...] = carry_ref[...] + jnp.sum(onehot.astype(F32), axis=1, keepdims=True)
    cnt_ref[...] = carry_ref[...]
    rid = lax.broadcasted_iota(I32, (8, tm), 0)
    rows = (e0, e1, s0 / tot, s1 / tot, r0, r1)
    info = jnp.zeros((8, tm), F32)
    for k, r in enumerate(rows):
        info = jnp.where(rid == k, r, info)
    info_ref[...] = info


def moe_router(x, a, b, router_w, router_b, *, tm=512):
    t, d = x.shape
    tm = min(tm, t)
    ne = N_EXPERTS
    return pl.pallas_call(
        functools.partial(_router_kernel, tm=tm),
        grid=(t // tm,),
        in_specs=[pl.BlockSpec((tm, d), lambda i: (i, 0)),
                  pl.BlockSpec((1, d), lambda i: (0, 0)),
                  pl.BlockSpec((1, d), lambda i: (0, 0)),
                  pl.BlockSpec((ne, d), lambda i: (0, 0)),
                  pl.BlockSpec((ne, 1), lambda i: (0, 0))],
        out_specs=[pl.BlockSpec((tm, d), lambda i: (i, 0)),
                   pl.BlockSpec((8, tm), lambda i: (0, i)),
                   pl.BlockSpec((ne, LANES), lambda i: (0, 0))],
        out_shape=[jax.ShapeDtypeStruct((t, d), F32),
                   jax.ShapeDtypeStruct((8, t), F32),
                   jax.ShapeDtypeStruct((ne, LANES), F32)],
        scratch_shapes=[pltpu.VMEM((ne, LANES), F32)],
        compiler_params=_params("arbitrary"),
        name="moe_router",
    )(x, a, b, router_w.T, router_b.reshape(ne, 1))


def _scatter_rows_kernel(pos_ref, h_ref, init_ref, o_ref, sem, *, tm):
    del init_ref
    base = pl.program_id(0) * tm

    def row_copy(r, k):
        dst = pos_ref[2 * (base + r) + k]
        return pltpu.make_async_copy(h_ref.at[pl.ds(r, 1), :], o_ref.at[pl.ds(dst, 1), :], sem)

    def start(r, carry):
        row_copy(r, 0).start()
        row_copy(r, 1).start()
        return carry

    def wait(r, carry):
        row_copy(r, 0).wait()
        row_copy(r, 1).wait()
        return carry

    lax.fori_loop(0, tm, start, 0)
    lax.fori_loop(0, tm, wait, 0)


def scatter_rows(h, pos_flat, n_rows, *, tm=256):
    t, d = h.shape
    tm = min(tm, t)
    return pl.pallas_call(
        functools.partial(_scatter_rows_kernel, tm=tm),
        grid_spec=pltpu.PrefetchScalarGridSpec(
            num_scalar_prefetch=1,
            grid=(t // tm,),
            in_specs=[pl.BlockSpec((tm, d), lambda i, pos: (i, 0)),
                      pl.BlockSpec(memory_space=pl.ANY)],
            out_specs=pl.BlockSpec(memory_space=pl.ANY),
            scratch_shapes=[pltpu.SemaphoreType.DMA(())]),
        out_shape=jax.ShapeDtypeStruct((n_rows, d), h.dtype),
        input_output_aliases={2: 0},
        compiler_params=_params("arbitrary"),
        name="moe_scatter",
    )(pos_flat, h, jnp.zeros((n_rows, d), h.dtype))


def _expert_up_kernel(te_ref, tv_ref, xs_ref, wg_ref, wu_ref, o_ref):
    i = pl.program_id(1)

    @pl.when(tv_ref[i] > 0)
    def _():
        x16 = xs_ref[...].astype(BF16)
        g = jnp.dot(x16, wg_ref[...], preferred_element_type=F32)
        u = jnp.dot(x16, wu_ref[...], preferred_element_type=F32)
        o_ref[...] = (g * jax.nn.sigmoid(g) * u).astype(o_ref.dtype)

    @pl.when(tv_ref[i] == 0)
    def _():
        o_ref[...] = jnp.zeros_like(o_ref)


def expert_up(xs, w_gate, w_up, tile_expert, tile_valid, *, tm, tf=512):
    p, d = xs.shape
    f = w_gate.shape[2]
    tf = min(tf, f)
    return pl.pallas_call(
        _expert_up_kernel,
        grid_spec=pltpu.PrefetchScalarGridSpec(
            num_scalar_prefetch=2,
            grid=(f // tf, p // tm),
            in_specs=[pl.BlockSpec((tm, d), lambda j, i, te, tv: (i, 0)),
                      pl.BlockSpec((None, d, tf), lambda j, i, te, tv: (te[i], 0, j)),
                      pl.BlockSpec((None, d, tf), lambda j, i, te, tv: (te[i], 0, j))],
            out_specs=pl.BlockSpec((tm, tf), lambda j, i, te, tv: (i, j))),
        out_shape=jax.ShapeDtypeStruct((p, f), BF16),
        compiler_params=_params("parallel", "parallel"),
        name="moe_expert_up",
    )(tile_expert, tile_valid, xs, w_gate, w_up)


def _expert_down_kernel(te_ref, tv_ref, a_ref, wd_ref, o_ref):
    i = pl.program_id(0)

    @pl.when(tv_ref[i] > 0)
    def _():
        o_ref[...] = jnp.dot(a_ref[...], wd_ref[...], preferred_element_type=F32)

    @pl.when(tv_ref[i] == 0)
    def _():
        o_ref[...] = jnp.zeros_like(o_ref)


def expert_down(act, w_down, tile_expert, tile_valid, *, tm):
    p, f = act.shape
    d = w_down.shape[2]
    return pl.pallas_call(
        _expert_down_kernel,
        grid_spec=pltpu.PrefetchScalarGridSpec(
            num_scalar_prefetch=2,
            grid=(p // tm,),
            in_specs=[pl.BlockSpec((tm, f), lambda i, te, tv: (i, 0)),
                      pl.BlockSpec((None, f, d), lambda i, te, tv: (te[i], 0, 0))],
            out_specs=pl.BlockSpec((tm, d), lambda i, te, tv: (i, 0))),
        out_shape=jax.ShapeDtypeStruct((p, d), F32),
        compiler_params=_params("parallel"),
        name="moe_expert_down",
    )(tile_expert, tile_valid, act, w_down)


def _combine_kernel(pos_ref, x_ref, g_ref, w_ref, y_ref, o_ref, buf_ref, sem, *, tm):
    base = pl.program_id(0) * tm

    def row_copy(r, k):
        src = pos_ref[2 * (base + r) + k]
        return pltpu.make_async_copy(y_ref.at[pl.ds(src, 1), :], buf_ref.at[k, pl.ds(r, 1), :], sem)

    def start(r, carry):
        row_copy(r, 0).start()
        row_copy(r, 1).start()
        return carry

    def wait(r, carry):
        row_copy(r, 0).wait()
        row_copy(r, 1).wait()
        return carry

    lax.fori_loop(0, tm, start, 0)
    lax.fori_loop(0, tm, wait, 0)
    w = w_ref[...]
    y = w[:, 0:1] * buf_ref[0] + w[:, 1:2] * buf_ref[1]
    o_ref[...] = x_ref[...] + g_ref[...] * y


def moe_combine(x, g, wts, y, pos_flat, *, tm=256):
    t, d = x.shape
    tm = min(tm, t)
    return pl.pallas_call(
        functools.partial(_combine_kernel, tm=tm),
        grid_spec=pltpu.PrefetchScalarGridSpec(
            num_scalar_prefetch=1,
            grid=(t // tm,),
            in_specs=[pl.BlockSpec((tm, d), lambda i, pos: (i, 0)),
                      pl.BlockSpec((1, d), lambda i, pos: (0, 0)),
                      pl.BlockSpec((tm, 2), lambda i, pos: (i, 0)),
                      pl.BlockSpec(memory_space=pl.ANY)],
            out_specs=pl.BlockSpec((tm, d), lambda i, pos: (i, 0)),
            scratch_shapes=[pltpu.VMEM((2, tm, d), F32), pltpu.SemaphoreType.DMA(())]),
        out_shape=jax.ShapeDtypeStruct((t, d), F32),
        compiler_params=_params("arbitrary"),
        name="moe_combine",
    )(pos_flat, x, g, wts, y)


MOE_ROW_TILE = 256


def moe_ffn_residual(x, a, b, g, router_w, router_b, w_gate, w_up, w_down):
    t, d = x.shape
    ne, tm = N_EXPERTS, MOE_ROW_TILE
    h, info, cnt = moe_router(x, a, b, router_w, router_b)
    counts = cnt[:, 0].astype(I32)
    padded = (counts + tm - 1) // tm * tm
    ends = jnp.cumsum(padded)
    starts = ends - padded
    n_tiles = 2 * t // tm + ne
    tile_row = jnp.arange(n_tiles, dtype=I32) * tm
    tile_expert = jnp.minimum(jnp.searchsorted(ends, tile_row, side="right"), ne - 1).astype(I32)
    tile_valid = (tile_row < ends[-1]).astype(I32)
    experts = info[0:2].astype(I32)
    pos = (starts[experts] + info[4:6].astype(I32)).T.reshape(-1)
    wts = info[2:4].T
    xs = scatter_rows(h, pos, n_tiles * tm)
    act = expert_up(xs, w_gate, w_up, tile_expert, tile_valid, tm=tm)
    y = expert_down(act, w_down, tile_expert, tile_valid, tm=tm)
    return moe_combine(x, g, wts, y, pos)


MOBA_W = MOBA_HEADS * HEAD_DIM
MLSTM_QK_W = MLSTM_HEADS * MLSTM_QK_DIM
MLSTM_V_W = MLSTM_HEADS * MLSTM_V_DIM
EVEN_MAIN_W = 3 * MOBA_W + 2 * MLSTM_QK_W + 2 * MLSTM_V_W
NSA_Q_W = NSA_HEADS * HEAD_DIM
NSA_KV_W = NSA_KV_HEADS * HEAD_DIM
NSA_GATE_W = 3 * NSA_HEADS
DIL_GROUP_W = DIL_HEADS * HEAD_DIM
DIL_W = len(DIL_PATTERNS) * DIL_GROUP_W
ODD_GATE_COL0 = NSA_Q_W + 6 * NSA_KV_W


def _pad_cols(w, width):
    return jnp.pad(w, ((0, 0), (0, width - w.shape[1])))


def moba_mlstm_mixer(x, a, b, g, w_in, w_out, qn_g, kn_g, i_b, f_b, out_g):
    w_main = w_in[:, :EVEN_MAIN_W].astype(BF16)
    w_gate = _pad_cols(w_in[:, EVEN_MAIN_W:], LANES).astype(BF16)
    z = norm_matmul(x, a, b, w_main)
    zg = norm_matmul(x, a, b, w_gate)
    kn, kmean = colprep(z, MOBA_W, MOBA_HEADS, kn_g, mean_rows=MOBA_BLOCK)
    vt = colprep(z, 2 * MOBA_W, MOBA_HEADS, transpose=True)
    o_a = moba_attention(z, 0, qn_g, kn, vt, kmean)
    c_q = 3 * MOBA_W
    c_k = c_q + MLSTM_QK_W
    c_v = c_k + MLSTM_QK_W
    c_o = c_v + MLSTM_V_W
    gate_bias = _pad_cols(jnp.concatenate([i_b, f_b]).reshape(1, -1), LANES)
    o_b = mlstm(z, c_q, c_k, c_v, c_o, zg, gate_bias, out_g)
    o = jnp.concatenate([o_a, o_b], axis=1)
    return matmul_residual(o, w_out.astype(BF16), x, g)


def nsa_dilated_mixer(x, a, b, g, w_in, w_out, nsa_qn_g, nsa_kn_g, pe_k, phik_w1, phik_b1, phik_w2,
                      pe_v, phiv_w1, phiv_b1, phiv_w2, dil_qn_g, dil_kn_g):
    t = x.shape[0]
    hd, kvw = HEAD_DIM, NSA_KV_W
    gc = ODD_GATE_COL0
    w_main = jnp.concatenate([w_in[:, :gc], w_in[:, gc + NSA_GATE_W:]], axis=1).astype(BF16)
    w_gate = _pad_cols(w_in[:, gc:gc + NSA_GATE_W], LANES).astype(BF16)
    z = norm_matmul(x, a, b, w_main)
    zg = norm_matmul(x, a, b, w_gate)
    r_heads = NSA_HEADS // NSA_KV_HEADS
    gates_t = zg[:, :NSA_GATE_W].reshape(t, NSA_KV_HEADS, r_heads, 3).transpose(1, 3, 2, 0)
    c_kc = NSA_Q_W

    def cmp_blocks(col0):
        s = NSA_CMP_STRIDE
        return (z[:, col0:col0 + kvw].reshape(t // s, s, NSA_KV_HEADS, hd)
                .transpose(2, 0, 1, 3).reshape(NSA_KV_HEADS, t // s, s * hd))

    kcmp = nsa_compress(cmp_blocks(c_kc), pe_k, phik_w1, phik_b1, phik_w2, nsa_kn_g, is_key=True)
    vcmp = nsa_compress(cmp_blocks(c_kc + kvw), pe_v, phiv_w1, phiv_b1, phiv_w2, nsa_kn_g, is_key=False)
    ksn = colprep(z, c_kc + 2 * kvw, NSA_KV_HEADS, nsa_kn_g)
    vs_t = colprep(z, c_kc + 3 * kvw, NSA_KV_HEADS, transpose=True)
    kwn = colprep(z, c_kc + 4 * kvw, NSA_KV_HEADS, nsa_kn_g)
    vw_t = colprep(z, c_kc + 5 * kvw, NSA_KV_HEADS, transpose=True)
    o_nsa = nsa_attention(z, 0, nsa_qn_g, gates_t, kcmp, vcmp, ksn, vs_t, kwn, vw_t)
    c_dq = gc
    n_dil = len(DIL_PATTERNS) * DIL_HEADS
    dqn = colprep(z, c_dq, n_dil, dil_qn_g, scale=hd ** -0.5)
    dkn = colprep(z, c_dq + DIL_W, n_dil, dil_kn_g)
    dvb = colprep(z, c_dq + 2 * DIL_W, n_dil)
    outs, lses = [], []
    for gi, (win, dil) in enumerate(DIL_PATTERNS):
        grp = lambda arr: arr[:, gi * DIL_GROUP_W:(gi + 1) * DIL_GROUP_W].reshape(t // dil, dil * DIL_GROUP_W)
        o_g, lse_g = dilated_group(grp(dqn), grp(dkn), grp(dvb), dil, back=win // dil)
        outs.append(o_g.reshape(t, DIL_GROUP_W))
        lses.append(lse_g.reshape(t, DIL_GROUP_W))
    o_dil = dilated_merge(outs, lses)
    o = jnp.concatenate([o_nsa, o_dil], axis=1)
    return matmul_residual(o, w_out.astype(BF16), x, g)


def kernel(x, c, ada_w, ada_b, norm_mix_g, norm_ffn_g, ev_w_in, ev_w_out, moba_qn_g, moba_kn_g, mlstm_i_b, mlstm_f_b, mlstm_out_g, od_w_in, od_w_out, nsa_qn_g, nsa_kn_g, nsa_pe_k, nsa_phik_w1, nsa_phik_b1, nsa_phik_w2, nsa_pe_v, nsa_phiv_w1, nsa_phiv_b1, nsa_phiv_w2, dil_qn_g, dil_kn_g, router_w, router_b, moe_w_gate, moe_w_up, moe_w_down):
    bsz, t, d = x.shape
    assert bsz == 1, "kernels are written for a single sequence"
    depth = ada_w.shape[0]
    mod = adaln(c, ada_w, ada_b)
    xs = x.reshape(t, d)
    for layer in range(depth):
        sh_m, sc_m, g_m, sh_f, sc_f, g_f = [m.reshape(1, d) for m in jnp.split(mod[layer], 6)]
        a_m = norm_mix_g[layer].reshape(1, d) * (1.0 + sc_m)
        a_f = norm_ffn_g[layer].reshape(1, d) * (1.0 + sc_f)
        j = layer // 2
        if layer % 2 == 0:
            xs = moba_mlstm_mixer(xs, a_m, sh_m, g_m, ev_w_in[j], ev_w_out[j], moba_qn_g[j], moba_kn_g[j],
                                  mlstm_i_b[j], mlstm_f_b[j], mlstm_out_g[j])
        else:
            xs = nsa_dilated_mixer(xs, a_m, sh_m, g_m, od_w_in[j], od_w_out[j], nsa_qn_g[j], nsa_kn_g[j],
                                   nsa_pe_k[j], nsa_phik_w1[j], nsa_phik_b1[j], nsa_phik_w2[j],
                                   nsa_pe_v[j], nsa_phiv_w1[j], nsa_phiv_b1[j], nsa_phiv_w2[j],
                                   dil_qn_g[j], dil_kn_g[j])
        xs = moe_ffn_residual(xs, a_f, sh_f, g_f, router_w, router_b, moe_w_gate[layer].astype(BF16),
                              moe_w_up[layer].astype(BF16), moe_w_down[layer].astype(BF16))
    return xs.reshape(bsz, t, d)
```

```python
import functools

import numpy as np
import jax
import jax.numpy as jnp
from jax import lax
from jax.experimental import pallas as pl
from jax.experimental.pallas import tpu as pltpu

F32 = jnp.float32
BF16 = jnp.bfloat16
I32 = jnp.int32

NORM_EPS = 1e-6
HEAD_DIM = 128
LANES = 128
VMEM_LIMIT_BYTES = 56 * 1024 * 1024
NEG = -1e30
LOG2E = 1.4426950408889634

MOBA_HEADS = 16
MOBA_BLOCK = 256
MOBA_TOPK = 3
MLSTM_HEADS = 4
MLSTM_QK_DIM = 256
MLSTM_V_DIM = 512
MLSTM_GATE_CAP = 15.0
NSA_HEADS = 16
NSA_KV_HEADS = 4
NSA_CMP_BLOCK = 32
NSA_CMP_STRIDE = 16
NSA_SLC_BLOCK = 64
NSA_SLC_TOPN = 16
NSA_WINDOW = 512
DIL_PATTERNS = ((128, 1), (512, 4), (2048, 16))
DIL_HEADS = 8
N_EXPERTS = 16
N_EXPERT_GROUPS = 4
EXPERTS_PER_GROUP = N_EXPERTS // N_EXPERT_GROUPS
D_FF_EXPERT = 1024

_NT = (((1,), (1,)), ((), ()))
_TN = (((0,), (0,)), ((), ()))


def _params(*sem):
    return pltpu.CompilerParams(dimension_semantics=sem, vmem_limit_bytes=VMEM_LIMIT_BYTES)


def _split_bf16(a):
    hi = a.astype(BF16)
    lo = (a - hi.astype(F32)).astype(BF16)
    return hi, lo


def _dot_nt_f32(a, b):
    a_hi, a_lo = _split_bf16(a)
    b_hi, b_lo = _split_bf16(b)
    d = functools.partial(lax.dot_general, dimension_numbers=_NT, preferred_element_type=F32)
    return d(a_hi, b_hi) + (d(a_hi, b_lo) + d(a_lo, b_hi))


def _rms(x, gain):
    return x * lax.rsqrt(jnp.mean(x * x, axis=-1, keepdims=True) + NORM_EPS) * gain


def _adaln_kernel(c_ref, w_ref, b_ref, o_ref):
    c = c_ref[...]
    ca = c * jax.nn.sigmoid(c)
    o_ref[...] = jnp.sum(w_ref[...] * ca, axis=0, keepdims=True) + b_ref[...]


def adaln(c, ada_w, ada_b):
    depth, d, n = ada_w.shape
    tn = 512
    out = pl.pallas_call(
        _adaln_kernel,
        grid=(depth, n // tn),
        in_specs=[pl.BlockSpec((d, 1), lambda l, j: (0, 0)),
                  pl.BlockSpec((None, d, tn), lambda l, j: (l, 0, j)),
                  pl.BlockSpec((None, 1, tn), lambda l, j: (l, 0, j))],
        out_specs=pl.BlockSpec((None, 1, tn), lambda l, j: (l, 0, j)),
        out_shape=jax.ShapeDtypeStruct((depth, 1, n), F32),
        compiler_params=_params("parallel", "parallel"),
        name="adaln",
    )(c.reshape(d, 1), ada_w, ada_b.reshape(depth, 1, n))
    return out.reshape(depth, n)


NORM_ROW_CHUNK = 128


def _norm_matmul_kernel(x_ref, a_ref, b_ref, w_ref, o_ref, h_ref):
    @pl.when(pl.program_id(1) == 0)
    def _():
        rows = min(NORM_ROW_CHUNK, x_ref.shape[0])

        def chunk(c, carry):
            sl = pl.ds(pl.multiple_of(c * rows, rows), rows)
            h_ref[sl, :] = (_rms(x_ref[sl, :], a_ref[...]) + b_ref[...]).astype(BF16)
            return carry

        lax.fori_loop(0, x_ref.shape[0] // rows, chunk, 0)

    o_ref[...] = jnp.dot(h_ref[...], w_ref[...], preferred_element_type=F32).astype(o_ref.dtype)


def norm_matmul(x, a, b, w, *, tm=1024, tn=512, out_dtype=F32):
    t, d = x.shape
    n = w.shape[1]
    tm, tn = min(tm, t), min(tn, n)
    return pl.pallas_call(
        _norm_matmul_kernel,
        grid=(t // tm, n // tn),
        in_specs=[pl.BlockSpec((tm, d), lambda i, j: (i, 0), pipeline_mode=pl.Buffered(1)),
                  pl.BlockSpec((1, d), lambda i, j: (0, 0)),
                  pl.BlockSpec((1, d), lambda i, j: (0, 0)),
                  pl.BlockSpec((d, tn), lambda i, j: (0, j))],
        out_specs=pl.BlockSpec((tm, tn), lambda i, j: (i, j)),
        out_shape=jax.ShapeDtypeStruct((t, n), out_dtype),
        scratch_shapes=[pltpu.VMEM((tm, d), BF16)],
        compiler_params=_params("parallel", "arbitrary"),
        name="norm_matmul",
    )(x, a, b, w)


def _matmul_residual_kernel(a_ref, w_ref, x_ref, g_ref, o_ref):
    y = jnp.dot(a_ref[...], w_ref[...], preferred_element_type=F32)
    o_ref[...] = x_ref[...] + g_ref[...] * y


def matmul_residual(a, w, x, g, *, tm=1024, tn=512):
    t, k = a.shape
    d = w.shape[1]
    tm, tn = min(tm, t), min(tn, d)
    return pl.pallas_call(
        _matmul_residual_kernel,
        grid=(t // tm, d // tn),
        in_specs=[pl.BlockSpec((tm, k), lambda i, j: (i, 0)),
                  pl.BlockSpec((k, tn), lambda i, j: (0, j)),
                  pl.BlockSpec((tm, tn), lambda i, j: (i, j)),
                  pl.BlockSpec((1, tn), lambda i, j: (0, j))],
        out_specs=pl.BlockSpec((tm, tn), lambda i, j: (i, j)),
        out_shape=jax.ShapeDtypeStruct((t, d), F32),
        compiler_params=_params("parallel", "parallel"),
        name="matmul_residual",
    )(a, w, x, g)


def _colprep_kernel(z_ref, g_ref, *o_refs, norm, scale, transpose, mean_rows):
    y = z_ref[...]
    if norm:
        y = _rms(y, g_ref[...])
    if mean_rows:
        tm = y.shape[0]
        o_refs[1][...] = jnp.mean(y.reshape(tm // mean_rows, mean_rows, y.shape[1]), axis=1)
    if scale != 1.0:
        y = y * scale
    o_refs[0][...] = (y.T if transpose else y).astype(o_refs[0].dtype)


def colprep(z, col0, nheads, gain=None, *, scale=1.0, transpose=False, mean_rows=0, tm=2048,
            out_dtype=BF16):
    t = z.shape[0]
    tm = min(tm, t)
    c0 = col0 // HEAD_DIM
    norm = gain is not None
    g = (gain if norm else jnp.ones((HEAD_DIM,), F32)).reshape(1, HEAD_DIM)
    w = nheads * HEAD_DIM
    if transpose:
        out_shape = [jax.ShapeDtypeStruct((w, t), out_dtype)]
        out_specs = [pl.BlockSpec((HEAD_DIM, tm), lambda i, h: (h, i))]
    else:
        out_shape = [jax.ShapeDtypeStruct((t, w), out_dtype)]
        out_specs = [pl.BlockSpec((tm, HEAD_DIM), lambda i, h: (i, h))]
    if mean_rows:
        out_shape.append(jax.ShapeDtypeStruct((t // mean_rows, w), F32))
        out_specs.append(pl.BlockSpec((tm // mean_rows, HEAD_DIM), lambda i, h: (i, h)))
    outs = pl.pallas_call(
        functools.partial(_colprep_kernel, norm=norm, scale=scale, transpose=transpose,
                          mean_rows=mean_rows),
        grid=(t // tm, nheads),
        in_specs=[pl.BlockSpec((tm, HEAD_DIM), lambda i, h: (i, c0 + h)),
                  pl.BlockSpec((1, HEAD_DIM), lambda i, h: (0, 0))],
        out_specs=out_specs,
        out_shape=out_shape,
        compiler_params=_params("parallel", "parallel"),
        name="colprep",
    )(z, g)
    return outs if mean_rows else outs[0]


def _flash_step(s_chunks, vt, m, l, acc_ref, slot):
    m_new = m
    for s in s_chunks:
        m_new = jnp.maximum(m_new, jnp.max(s, axis=0, keepdims=True))
    alpha = jnp.exp2(m - m_new)
    ps = [jnp.exp2(s - m_new) for s in s_chunks]
    l_new = alpha * l
    for p in ps:
        l_new = l_new + jnp.sum(p, axis=0, keepdims=True)
    p16 = jnp.concatenate([p.astype(BF16) for p in ps], axis=0) if len(ps) > 1 else ps[0].astype(BF16)
    acc_ref[slot] = alpha * acc_ref[slot] + jnp.dot(vt, p16, preferred_element_type=F32)
    return m_new, l_new


def _moba_kernel(zq_ref, g_ref, kn_ref, vt_ref, km_ref, o_ref, bias_ref, acc_ref, *, tq, nb, topk):
    i = pl.program_id(1)
    qn = _rms(zq_ref[...], g_ref[...])
    qs = (qn * (HEAD_DIM ** -0.5 * LOG2E)).astype(BF16)
    pos = i * tq + lax.broadcasted_iota(I32, (1, tq), 1)
    cur = (pos // MOBA_BLOCK).astype(F32)
    bidx = lax.broadcasted_iota(I32, (nb, tq), 0).astype(F32)

    gate = jnp.where(bidx < cur, _dot_nt_f32(km_ref[...], qn), -jnp.inf)
    sel = jnp.zeros((nb, tq), F32)
    for _ in range(topk):
        mx = jnp.max(gate, axis=0, keepdims=True)
        is_max = (gate == mx) & (gate > -jnp.inf)
        first = jnp.min(jnp.where(is_max, bidx, float(nb)), axis=0, keepdims=True)
        pick = bidx == first
        sel = jnp.where(pick, 1.0, sel)
        gate = jnp.where(pick, -jnp.inf, gate)
    sel = jnp.where(bidx == cur, 1.0, sel)
    bias_ref[...] = jnp.where(sel > 0.5, 0.0, NEG)
    acc_ref[...] = jnp.zeros_like(acc_ref)
    nblk = tq // MOBA_BLOCK

    def tile(off, blk0, causal, m, l):
        chunks = []
        for b in range(nblk):
            k = kn_ref[pl.ds(off + b * MOBA_BLOCK, MOBA_BLOCK), :]
            s = lax.dot_general(k, qs, _NT, preferred_element_type=F32) + bias_ref[pl.ds(blk0 + b, 1), :]
            if causal:
                kpos = off + b * MOBA_BLOCK + lax.broadcasted_iota(I32, (MOBA_BLOCK, 1), 0)
                s = jnp.where(kpos <= pos, s, NEG)
            chunks.append(s)
        return _flash_step(chunks, vt_ref[:, pl.ds(off, tq)], m, l, acc_ref, 0)

    def body(j, carry):
        return tile(pl.multiple_of(j * tq, tq), j * nblk, False, *carry)

    carry = lax.fori_loop(0, i, body, (jnp.full((1, tq), NEG, F32), jnp.zeros((1, tq), F32)))
    _, l = tile(pl.multiple_of(i * tq, tq), i * nblk, True, *carry)
    o_ref[...] = (acc_ref[0] / l).T.astype(o_ref.dtype)


def moba_attention(z, q_col0, q_gain, kn, vt, kmean, *, tq=1024):
    t = z.shape[0]
    tq = min(tq, t)
    nb = t // MOBA_BLOCK
    h = MOBA_HEADS
    c0 = q_col0 // HEAD_DIM
    return pl.pallas_call(
        functools.partial(_moba_kernel, tq=tq, nb=nb, topk=min(MOBA_TOPK, nb)),
        grid=(h, t // tq),
        in_specs=[pl.BlockSpec((tq, HEAD_DIM), lambda hh, i: (i, c0 + hh)),
                  pl.BlockSpec((1, HEAD_DIM), lambda hh, i: (0, 0)),
                  pl.BlockSpec((t, HEAD_DIM), lambda hh, i: (0, hh)),
                  pl.BlockSpec((HEAD_DIM, t), lambda hh, i: (hh, 0)),
                  pl.BlockSpec((nb, HEAD_DIM), lambda hh, i: (0, hh))],
        out_specs=pl.BlockSpec((tq, HEAD_DIM), lambda hh, i: (i, hh)),
        out_shape=jax.ShapeDtypeStruct((t, h * HEAD_DIM), BF16),
        scratch_shapes=[pltpu.VMEM((nb, tq), F32), pltpu.VMEM((1, HEAD_DIM, tq), F32)],
        compiler_params=_params("parallel", "parallel"),
        name="moba",
    )(z, q_gain.reshape(1, HEAD_DIM), kn, vt, kmean)


def _mlstm_kernel(zq_ref, zk_ref, zv_ref, zo_ref, zg_ref, gb_ref, og_ref, o_ref, ct_ref, n_ref, m_ref,
                  *, chunk):
    dk, dv, cap = MLSTM_QK_DIM, MLSTM_V_DIM, MLSTM_GATE_CAP

    @pl.when(pl.program_id(0) == 0)
    def _():
        ct_ref[...] = jnp.zeros_like(ct_ref)
        n_ref[...] = jnp.zeros_like(n_ref)
        m_ref[...] = jnp.zeros_like(m_ref)

    pre = cap * jnp.tanh((zg_ref[...] + gb_ref[...]) / cap)
    log_f = jnp.minimum(pre, 0.0) - jnp.log1p(jnp.exp(-jnp.abs(pre)))
    row = lax.broadcasted_iota(I32, (chunk, chunk), 0)
    col = lax.broadcasted_iota(I32, (chunk, chunk), 1)
    causal = col <= row
    tri = causal.astype(BF16)
    p1 = log_f.astype(BF16)
    r1 = log_f - p1.astype(F32)
    p2 = r1.astype(BF16)
    p3 = (r1 - p2.astype(F32)).astype(BF16)
    mm = functools.partial(jnp.dot, preferred_element_type=F32)
    b = mm(tri, p1) + (mm(tri, p2) + mm(tri, p3))
    pre_t = pre.T
    b_t = b.T

    for h in range(MLSTM_HEADS):
        li_row = pre_t[h:h + 1, :]
        li_col = pre[:, h:h + 1]
        b_row = b_t[MLSTM_HEADS + h:MLSTM_HEADS + h + 1, :]
        b_col = b[:, MLSTM_HEADS + h:MLSTM_HEADS + h + 1]
        m_prev = m_ref[h][:, :1]
        log_d = jnp.where(causal, b_col - b_row + li_row, -jnp.inf)
        log_inter = b_col + m_prev
        m_s = jnp.maximum(log_inter, jnp.max(log_d, axis=1, keepdims=True))
        dmat = jnp.exp(log_d - m_s)
        inter_w = jnp.exp(log_inter - m_s)
        qf = zq_ref[:, h * dk:(h + 1) * dk] * dk ** -0.5
        kf = zk_ref[:, h * dk:(h + 1) * dk]
        q16 = qf.astype(BF16)
        v16 = zv_ref[:, h * dv:(h + 1) * dv].astype(BF16)
        s = lax.dot_general(q16, kf.astype(BF16), _NT, preferred_element_type=F32) * dmat
        ct = ct_ref[h]
        num = mm(s.astype(BF16), v16) + inter_w * mm(q16, ct.astype(BF16))
        den = (jnp.sum(s, axis=1, keepdims=True)
               + inter_w * jnp.sum(qf * n_ref[h], axis=1, keepdims=True))
        hh = num / jnp.maximum(jnp.abs(den), jnp.exp(-m_s))
        hn = _rms(hh, og_ref[:, h * dv:(h + 1) * dv])
        o_ref[:, h * dv:(h + 1) * dv] = (hn * jax.nn.sigmoid(zo_ref[:, h * dv:(h + 1) * dv])).astype(o_ref.dtype)
        b_end = b_col[chunk - 1:chunk, :]
        log_w = b_end - b_col + li_col
        m_new = jnp.maximum(b_end + m_prev, jnp.max(log_w, axis=0, keepdims=True))
        kw = kf * jnp.exp(log_w - m_new)
        decay = jnp.exp(b_end + m_prev - m_new)
        ct_ref[h] = decay * ct + lax.dot_general(kw.astype(BF16), v16, _TN, preferred_element_type=F32)
        n_ref[h] = decay * n_ref[h] + jnp.sum(kw, axis=0, keepdims=True)
        m_ref[h] = jnp.broadcast_to(m_new, (1, LANES))


def mlstm(z, q_col0, k_col0, v_col0, o_col0, zg, gate_bias, out_gain, *, chunk=256):
    t = z.shape[0]
    chunk = min(chunk, t)
    wk, wv = MLSTM_HEADS * MLSTM_QK_DIM, MLSTM_HEADS * MLSTM_V_DIM
    return pl.pallas_call(
        functools.partial(_mlstm_kernel, chunk=chunk),
        grid=(t // chunk,),
        in_specs=[pl.BlockSpec((chunk, wk), lambda c: (c, q_col0 // wk)),
                  pl.BlockSpec((chunk, wk), lambda c: (c, k_col0 // wk)),
                  pl.BlockSpec((chunk, wv), lambda c: (c, v_col0 // wv)),
                  pl.BlockSpec((chunk, wv), lambda c: (c, o_col0 // wv)),
                  pl.BlockSpec((chunk, LANES), lambda c: (c, 0)),
                  pl.BlockSpec((1, LANES), lambda c: (0, 0)),
                  pl.BlockSpec((1, wv), lambda c: (0, 0))],
        out_specs=pl.BlockSpec((chunk, wv), lambda c: (c, 0)),
        out_shape=jax.ShapeDtypeStruct((t, wv), BF16),
        scratch_shapes=[pltpu.VMEM((MLSTM_HEADS, MLSTM_QK_DIM, MLSTM_V_DIM), F32),
                        pltpu.VMEM((MLSTM_HEADS, 1, MLSTM_QK_DIM), F32),
                        pltpu.VMEM((MLSTM_HEADS, 1, LANES), F32)],
        compiler_params=_params("arbitrary"),
        name="mlstm",
    )(z, z, z, z, zg, gate_bias, out_gain.reshape(1, wv))


def _nsa_compress_kernel(a_ref, pe_ref, w1_ref, b1_ref, w2_ref, g_ref, o_ref, *, is_key):
    n, half = a_ref.shape
    w1 = w1_ref[...]
    bias = jnp.sum(w1 * pe_ref[...], axis=0, keepdims=True) + b1_ref[...]
    w1 = w1.astype(BF16)
    a16 = a_ref[...].astype(BF16)
    u = jnp.dot(a16, w1[:half], preferred_element_type=F32)
    v = jnp.dot(a16, w1[half:], preferred_element_type=F32)
    x = u + pltpu.roll(v, n - 1, 0) + bias
    hid = 0.5 * x * (1.0 + jnp.tanh(np.sqrt(2.0 / np.pi) * (x + 0.044715 * (x * x * x))))
    y = jnp.dot(hid.astype(BF16), w2_ref[...].astype(BF16), preferred_element_type=F32)
    if is_key:
        y = _rms(y, g_ref[...])
    o_ref[...] = y.astype(o_ref.dtype)


def nsa_compress(a, pe, w1, b1, w2, gain, *, is_key):
    g, n, half = a.shape
    hd = HEAD_DIM
    out_shape, out_spec = jax.ShapeDtypeStruct((g, n, hd), BF16), pl.BlockSpec((None, n, hd), lambda i: (i, 0, 0))
    full = lambda shape: pl.BlockSpec(shape, lambda i: (0,) * len(shape))
    return pl.pallas_call(
        functools.partial(_nsa_compress_kernel, is_key=is_key),
        grid=(g,),
        in_specs=[pl.BlockSpec((None, n, half), lambda i: (i, 0, 0)),
                  full((2 * half, 1)), full((2 * half, hd)), full((1, hd)), full((hd, hd)), full((1, hd))],
        out_specs=out_spec,
        out_shape=out_shape,
        compiler_params=_params("parallel"),
        name="nsa_compress",
    )(a, pe.reshape(2 * half, 1), w1, b1.reshape(1, hd), w2, gain.reshape(1, hd))


NSA_SLC_KEY_TILE = 1024
NSA_SLC_SUB_KEYS = 256


def _nsa_kernel(zq_ref, qg_ref, gate_ref, kc_ref, vc_ref, mt_ref, ks_ref, vst_ref, kw_ref, vwt_ref,
                o_ref, bias_ref, acc_ref, *, tq, nc, ns, ntop, kt):
    i = pl.program_id(1)
    r_heads = NSA_HEADS // NSA_KV_HEADS
    w = r_heads * tq
    hd = HEAD_DIM
    scale = hd ** -0.5 * LOG2E
    qs = jnp.concatenate(
        [(_rms(zq_ref[:, r * hd:(r + 1) * hd], qg_ref[...]) * scale).astype(BF16) for r in range(r_heads)],
        axis=0)
    pos_t = i * tq + lax.broadcasted_iota(I32, (1, tq), 1)
    pos = i * tq + (lax.broadcasted_iota(I32, (1, w), 1) & (tq - 1))

    s = lax.dot_general(kc_ref[...], qs, _NT, preferred_element_type=F32)
    cmp_end = lax.broadcasted_iota(I32, (nc, 1), 0) * NSA_CMP_STRIDE + (NSA_CMP_BLOCK - 1)
    cmask = cmp_end <= pos
    s = jnp.where(cmask, s, NEG)
    p = jnp.where(cmask, jnp.exp2(s - jnp.max(s, axis=0, keepdims=True)), 0.0)
    den = jnp.sum(p, axis=0, keepdims=True)
    p = p * (1.0 / jnp.where(den > 0.0, den, 1.0))
    o_cmp = lax.dot_general(vc_ref[...], p.astype(BF16), _TN, preferred_element_type=F32)
    imp = p[:, 0:tq]
    for r in range(1, r_heads):
        imp = imp + p[:, r * tq:(r + 1) * tq]
    imp_hi, imp_lo = _split_bf16(imp)
    p_slc = (jnp.dot(mt_ref[...], imp_hi, preferred_element_type=F32)
             + jnp.dot(mt_ref[...], imp_lo, preferred_element_type=F32))

    blk = lax.broadcasted_iota(I32, (ns, tq), 0).astype(F32)
    cur = (pos_t // NSA_SLC_BLOCK).astype(F32)
    forced = (blk == 0.0) | (blk == cur) | (blk == cur - 1.0)
    score0 = jnp.where(blk > cur, -jnp.inf, jnp.where(forced, jnp.inf, p_slc))

    def pick_one(_, carry):
        score, sel = carry
        mx = jnp.max(score, axis=0, keepdims=True)
        is_max = (score == mx) & (score > -jnp.inf)
        first = jnp.min(jnp.where(is_max, blk, float(ns)), axis=0, keepdims=True)
        pick = blk == first
        return jnp.where(pick, -jnp.inf, score), jnp.where(pick, 1.0, sel)

    _, sel = lax.fori_loop(0, ntop, pick_one, (score0, jnp.zeros((ns, tq), F32)))
    bias = jnp.where(sel > 0.5, 0.0, NEG)
    bias_ref[...] = jnp.concatenate([bias] * r_heads, axis=1)
    acc_ref[...] = jnp.zeros_like(acc_ref)

    sub = NSA_SLC_SUB_KEYS
    per_sub = sub // NSA_SLC_BLOCK

    def slc_tile(off, blk0, causal, m, l):
        chunks = []
        for c in range(kt // sub):
            s = lax.dot_general(ks_ref[pl.ds(off + c * sub, sub), :], qs, _NT, preferred_element_type=F32)
            for b in range(per_sub):
                r = c * per_sub + b
                sb = s[b * NSA_SLC_BLOCK:(b + 1) * NSA_SLC_BLOCK, :] + bias_ref[pl.ds(blk0 + r, 1), :]
                if causal:
                    kpos = off + r * NSA_SLC_BLOCK + lax.broadcasted_iota(I32, (NSA_SLC_BLOCK, 1), 0)
                    sb = jnp.where(kpos <= pos, sb, NEG)
                chunks.append(sb)
        return _flash_step(chunks, vst_ref[:, pl.ds(off, kt)], m, l, acc_ref, 0)

    def slc_body(j, carry):
        return slc_tile(pl.multiple_of(j * kt, kt), j * (kt // NSA_SLC_BLOCK), False, *carry)

    n_past = (i * tq) // kt
    carry = lax.fori_loop(0, n_past, slc_body, (jnp.full((1, w), NEG, F32), jnp.zeros((1, w), F32)))
    _, l_slc = slc_tile(pl.multiple_of(n_past * kt, kt), n_past * (kt // NSA_SLC_BLOCK), True, *carry)

    wk = NSA_WINDOW + tq
    off = pl.multiple_of(jnp.maximum(i * tq - NSA_WINDOW, 0), tq)
    s = lax.dot_general(kw_ref[pl.ds(off, wk), :], qs, _NT, preferred_element_type=F32)
    kpos = off + lax.broadcasted_iota(I32, (wk, 1), 0)
    s = jnp.where((kpos <= pos) & (kpos > pos - NSA_WINDOW), s, NEG)
    p = jnp.exp2(s - jnp.max(s, axis=0, keepdims=True))
    o_win = (jnp.dot(vwt_ref[:, pl.ds(off, wk)], p.astype(BF16), preferred_element_type=F32)
             / jnp.sum(p, axis=0, keepdims=True))

    gates = jax.nn.sigmoid(gate_ref[...])
    o_slc = acc_ref[0] / l_slc
    for r in range(r_heads):
        cols = slice(r * tq, (r + 1) * tq)
        o = (gates[0, r:r + 1, :] * o_cmp[:, cols] + gates[1, r:r + 1, :] * o_slc[:, cols]
             + gates[2, r:r + 1, :] * o_win[:, cols])
        o_ref[:, r * hd:(r + 1) * hd] = o.T.astype(o_ref.dtype)


def nsa_slc_weights(ns, nc):
    ratio = NSA_SLC_BLOCK // NSA_CMP_STRIDE
    m = np.zeros((ns, nc), np.float32)
    for j in range(ns):
        for d, wgt in [(-1, 0.5)] + [(k, 1.0) for k in range(ratio - 1)] + [(ratio - 1, 0.5)]:
            n = ratio * j + d
            if 0 <= n < nc - 1:
                m[j, n] = wgt
    return m


def nsa_attention(z, q_col0, q_gain, gates_t, kcmp, vcmp, ksn, vs_t, kwn, vw_t, *, tq=128):
    t = z.shape[0]
    g, r_heads, hd = NSA_KV_HEADS, NSA_HEADS // NSA_KV_HEADS, HEAD_DIM
    nc, ns = t // NSA_CMP_STRIDE, t // NSA_SLC_BLOCK
    mt = jnp.asarray(nsa_slc_weights(ns, nc), BF16)
    qw = r_heads * hd
    once = pl.Buffered(1)
    return pl.pallas_call(
        functools.partial(_nsa_kernel, tq=tq, nc=nc, ns=ns, ntop=min(NSA_SLC_TOPN, ns),
                          kt=min(NSA_SLC_KEY_TILE, t)),
        grid=(g, t // tq),
        in_specs=[pl.BlockSpec((tq, qw), lambda gg, i: (i, q_col0 // qw + gg)),
                  pl.BlockSpec((1, hd), lambda gg, i: (0, 0)),
                  pl.BlockSpec((None, 3, r_heads, tq), lambda gg, i: (gg, 0, 0, i)),
                  pl.BlockSpec((None, nc, hd), lambda gg, i: (gg, 0, 0)),
                  pl.BlockSpec((None, nc, hd), lambda gg, i: (gg, 0, 0)),
                  pl.BlockSpec((ns, nc), lambda gg, i: (0, 0)),
                  pl.BlockSpec((t, hd), lambda gg, i: (0, gg), pipeline_mode=once),
                  pl.BlockSpec((hd, t), lambda gg, i: (gg, 0), pipeline_mode=once),
                  pl.BlockSpec((t, hd), lambda gg, i: (0, gg), pipeline_mode=once),
                  pl.BlockSpec((hd, t), lambda gg, i: (gg, 0), pipeline_mode=once)],
        out_specs=pl.BlockSpec((tq, qw), lambda gg, i: (i, gg)),
        out_shape=jax.ShapeDtypeStruct((t, NSA_HEADS * hd), BF16),
        scratch_shapes=[pltpu.VMEM((ns, r_heads * tq), F32), pltpu.VMEM((1, hd, r_heads * tq), F32)],
        compiler_params=_params("parallel", "arbitrary"),
        name="nsa",
    )(z, q_gain.reshape(1, hd), gates_t, kcmp, vcmp, mt, ksn, vs_t, kwn, vw_t)


def _dilated_kernel(q_ref, kp_ref, kc_ref, vp_ref, vc_ref, o_ref, lse_ref, *, tq, back):
    i = pl.program_id(1)
    hd = HEAD_DIM
    qpos = i * tq + lax.broadcasted_iota(I32, (tq, 1), 0)
    kpos = (i - 1) * tq + lax.broadcasted_iota(I32, (1, 2 * tq), 1)
    mask = (kpos <= qpos) & (kpos >= qpos - back) & (kpos >= 0)
    for j in range(DIL_HEADS):
        cols = slice(j * hd, (j + 1) * hd)
        k = jnp.concatenate([kp_ref[:, cols], kc_ref[:, cols]], axis=0)
        v = jnp.concatenate([vp_ref[:, cols], vc_ref[:, cols]], axis=0)
        s = lax.dot_general(q_ref[:, cols], k, _NT, preferred_element_type=F32)
        s = jnp.where(mask, s, NEG)
        m = jnp.max(s, axis=1, keepdims=True)
        p = jnp.exp(s - m)
        l = jnp.sum(p, axis=1, keepdims=True)
        o_ref[:, cols] = jnp.dot(p.astype(BF16), v, preferred_element_type=F32) / l
        lse_ref[:, cols] = jnp.broadcast_to(m + jnp.log(l), (tq, hd))


def dilated_group(qd, kd, vd, dil, *, back, tq=128):
    td, wd = qd.shape
    hw = wd // dil
    tq = min(tq, td)
    cur = pl.BlockSpec((tq, hw), lambda r, i: (i, r))
    prev = pl.BlockSpec((tq, hw), lambda r, i: (jnp.maximum(i - 1, 0), r))
    return pl.pallas_call(
        functools.partial(_dilated_kernel, tq=tq, back=back),
        grid=(dil, td // tq),
        in_specs=[cur, prev, cur, prev, cur],
        out_specs=[cur, cur],
        out_shape=[jax.ShapeDtypeStruct((td, wd), F32), jax.ShapeDtypeStruct((td, wd), F32)],
        compiler_params=_params("parallel", "parallel"),
        name="dilated",
    )(qd, kd, kd, vd, vd)


def _dilated_merge_kernel(*refs):
    n = (len(refs) - 1) // 2
    o_refs, lse_refs, out_ref = refs[:n], refs[n:2 * n], refs[2 * n]
    lse = [r[...] for r in lse_refs]
    m = functools.reduce(jnp.maximum, lse)
    e = [jnp.exp(x - m) for x in lse]
    tot = functools.reduce(lambda a, b: a + b, e)
    acc = sum(ei * r[...] for ei, r in zip(e, o_refs))
    out_ref[...] = (acc / tot).astype(out_ref.dtype)


def dilated_merge(outs, lses, *, tm=512):
    t, wd = outs[0].shape
    tm = min(tm, t)
    spec = pl.BlockSpec((tm, wd), lambda i: (i, 0))
    return pl.pallas_call(
        _dilated_merge_kernel,
        grid=(t // tm,),
        in_specs=[spec] * (2 * len(outs)),
        out_specs=spec,
        out_shape=jax.ShapeDtypeStruct((t, wd), BF16),
        compiler_params=_params("parallel"),
        name="dilated_merge",
    )(*outs, *lses)


def _top2_of_4(v):
    best, loc0 = v[0], jnp.zeros_like(v[0])
    for j in range(1, 4):
        better = v[j] > best
        best = jnp.where(better, v[j], best)
        loc0 = jnp.where(better, float(j), loc0)
    best1, loc1 = jnp.full_like(v[0], -jnp.inf), jnp.zeros_like(v[0])
    for j in range(4):
        better = (v[j] > best1) & (loc0 != float(j))
        best1 = jnp.where(better, v[j], best1)
        loc1 = jnp.where(better, float(j), loc1)
    return loc0, loc1


def _router_kernel(x_ref, a_ref, b_ref, wt_ref, rb_ref, h_ref, info_ref, cnt_ref, carry_ref, *, tm):
    ne, ng, eg = N_EXPERTS, N_EXPERT_GROUPS, EXPERTS_PER_GROUP

    @pl.when(pl.program_id(0) == 0)
    def _():
        carry_ref[...] = jnp.zeros_like(carry_ref)

    h = _rms(x_ref[...], a_ref[...]) + b_ref[...]
    h_ref[...] = h
    s = jax.nn.sigmoid(_dot_nt_f32(wt_ref[...], h))
    sel = s + rb_ref[...]
    srow = [s[e:e + 1, :] for e in range(ne)]
    selrow = [sel[e:e + 1, :] for e in range(ne)]
    gscore = []
    for g in range(ng):
        a0, a1, a2, a3 = selrow[eg * g:eg * g + 4]
        hi1, lo1, hi2, lo2 = jnp.maximum(a0, a1), jnp.minimum(a0, a1), jnp.maximum(a2, a3), jnp.minimum(a2, a3)
        gscore.append(jnp.maximum(hi1, hi2) + jnp.maximum(jnp.minimum(hi1, hi2), jnp.maximum(lo1, lo2)))
    best, grp = gscore[0], jnp.zeros_like(gscore[0])
    for g in range(1, ng):
        better = gscore[g] > best
        best = jnp.where(better, gscore[g], best)
        grp = jnp.where(better, float(g), grp)
    pick = lambda rows, j: sum(jnp.where(grp == float(g), rows[eg * g + j], 0.0) for g in range(ng))
    loc0, loc1 = _top2_of_4([pick(selrow, j) for j in range(eg)])
    s_in = [pick(srow, j) for j in range(eg)]
    s0 = sum(jnp.where(loc0 == float(j), s_in[j], 0.0) for j in range(eg))
    s1 = sum(jnp.where(loc1 == float(j), s_in[j], 0.0) for j in range(eg))
    e0 = grp * eg + loc0
    e1 = grp * eg + loc1
    tot = s0 + s1
    eidx = lax.broadcasted_iota(I32, (ne, tm), 0).astype(F32)
    oh0 = eidx == e0
    oh1 = eidx == e1
    onehot = (oh0 | oh1).astype(BF16)
    upper = (lax.broadcasted_iota(I32, (tm, tm), 0) < lax.broadcasted_iota(I32, (tm, tm), 1)).astype(BF16)
    before = jnp.dot(onehot, upper, preferred_element_type=F32) + carry_ref[:, :1]
    r0 = jnp.sum(jnp.where(oh0, before, 0.0), axis=0, keepdims=True)
    r1 = jnp.sum(jnp.where(oh1, before, 0.0), axis=0, keepdims=True)
    carry_ref[...] = carry_ref[...] + jnp.sum(onehot.astype(F32), axis=1, keepdims=True)
    cnt_ref[...] = carry_ref[...]
    rid = lax.broadcasted_iota(I32, (8, tm), 0)
    rows = (e0, e1, s0 / tot, s1 / tot, r0, r1)
    info = jnp.zeros((8, tm), F32)
    for k, r in enumerate(rows):
        info = jnp.where(rid == k, r, info)
    info_ref[...] = info


def moe_router(x, a, b, router_w, router_b, *, tm=512):
    t, d = x.shape
    tm = min(tm, t)
    ne = N_EXPERTS
    return pl.pallas_call(
        functools.partial(_router_kernel, tm=tm),
        grid=(t // tm,),
        in_specs=[pl.BlockSpec((tm, d), lambda i: (i, 0)),
                  pl.BlockSpec((1, d), lambda i: (0, 0)),
                  pl.BlockSpec((1, d), lambda i: (0, 0)),
                  pl.BlockSpec((ne, d), lambda i: (0, 0)),
                  pl.BlockSpec((ne, 1), lambda i: (0, 0))],
        out_specs=[pl.BlockSpec((tm, d), lambda i: (i, 0)),
                   pl.BlockSpec((8, tm), lambda i: (0, i)),
                   pl.BlockSpec((ne, LANES), lambda i: (0, 0))],
        out_shape=[jax.ShapeDtypeStruct((t, d), F32),
                   jax.ShapeDtypeStruct((8, t), F32),
                   jax.ShapeDtypeStruct((ne, LANES), F32)],
        scratch_shapes=[pltpu.VMEM((ne, LANES), F32)],
        compiler_params=_params("arbitrary"),
        name="moe_router",
    )(x, a, b, router_w.T, router_b.reshape(ne, 1))


def _scatter_rows_kernel(pos_ref, h_ref, init_ref, o_ref, sem, *, tm):
    del init_ref
    base = pl.program_id(0) * tm

    def row_copy(r, k):
        dst = pos_ref[2 * (base + r) + k]
        return pltpu.make_async_copy(h_ref.at[pl.ds(r, 1), :], o_ref.at[pl.ds(dst, 1), :], sem)

    def start(r, carry):
        row_copy(r, 0).start()
        row_copy(r, 1).start()
        return carry

    def wait(r, carry):
        row_copy(r, 0).wait()
        row_copy(r, 1).wait()
        return carry

    lax.fori_loop(0, tm, start, 0)
    lax.fori_loop(0, tm, wait, 0)


def scatter_rows(h, pos_flat, n_rows, *, tm=256):
    t, d = h.shape
    tm = min(tm, t)
    return pl.pallas_call(
        functools.partial(_scatter_rows_kernel, tm=tm),
        grid_spec=pltpu.PrefetchScalarGridSpec(
            num_scalar_prefetch=1,
            grid=(t // tm,),
            in_specs=[pl.BlockSpec((tm, d), lambda i, pos: (i, 0)),
                      pl.BlockSpec(memory_space=pl.ANY)],
            out_specs=pl.BlockSpec(memory_space=pl.ANY),
            scratch_shapes=[pltpu.SemaphoreType.DMA(())]),
        out_shape=jax.ShapeDtypeStruct((n_rows, d), h.dtype),
        input_output_aliases={2: 0},
        compiler_params=_params("arbitrary"),
        name="moe_scatter",
    )(pos_flat, h, jnp.zeros((n_rows, d), h.dtype))


def _expert_changed(te_ref, i):
    return (i == 0) | (te_ref[i] != te_ref[jnp.maximum(i - 1, 0)])


def _expert_up_kernel(te_ref, tv_ref, xs_ref, wg_ref, wu_ref, o_ref, wg16_ref, wu16_ref):
    i = pl.program_id(1)

    @pl.when(_expert_changed(te_ref, i))
    def _():
        wg16_ref[...] = wg_ref[...].astype(BF16)
        wu16_ref[...] = wu_ref[...].astype(BF16)

    @pl.when(tv_ref[i] > 0)
    def _():
        x16 = xs_ref[...].astype(BF16)
        g = jnp.dot(x16, wg16_ref[...], preferred_element_type=F32)
        u = jnp.dot(x16, wu16_ref[...], preferred_element_type=F32)
        o_ref[...] = (g * jax.nn.sigmoid(g) * u).astype(o_ref.dtype)

    @pl.when(tv_ref[i] == 0)
    def _():
        o_ref[...] = jnp.zeros_like(o_ref)


def expert_up(xs, w_gate, w_up, layer, tile_expert, tile_valid, *, tm, tf=512):
    p, d = xs.shape
    f = w_gate.shape[3]
    tf = min(tf, f)
    w_spec = pl.BlockSpec((None, None, d, tf), lambda j, i, te, tv: (layer, te[i], 0, j),
                          pipeline_mode=pl.Buffered(1))
    return pl.pallas_call(
        _expert_up_kernel,
        grid_spec=pltpu.PrefetchScalarGridSpec(
            num_scalar_prefetch=2,
            grid=(f // tf, p // tm),
            in_specs=[pl.BlockSpec((tm, d), lambda j, i, te, tv: (i, 0)), w_spec, w_spec],
            out_specs=pl.BlockSpec((tm, tf), lambda j, i, te, tv: (i, j)),
            scratch_shapes=[pltpu.VMEM((d, tf), BF16), pltpu.VMEM((d, tf), BF16)]),
        out_shape=jax.ShapeDtypeStruct((p, f), BF16),
        compiler_params=_params("arbitrary", "arbitrary"),
        name="moe_expert_up",
    )(tile_expert, tile_valid, xs, w_gate, w_up)


def _expert_down_kernel(te_ref, tv_ref, a_ref, wd_ref, o_ref, wd16_ref):
    i = pl.program_id(1)

    @pl.when(_expert_changed(te_ref, i))
    def _():
        wd16_ref[...] = wd_ref[...].astype(BF16)

    @pl.when(tv_ref[i] > 0)
    def _():
        o_ref[...] = jnp.dot(a_ref[...], wd16_ref[...], preferred_element_type=F32)

    @pl.when(tv_ref[i] == 0)
    def _():
        o_ref[...] = jnp.zeros_like(o_ref)


def expert_down(act, w_down, layer, tile_expert, tile_valid, *, tm, tn=2048):
    p, f = act.shape
    d = w_down.shape[3]
    tn = min(tn, d)
    return pl.pallas_call(
        _expert_down_kernel,
        grid_spec=pltpu.PrefetchScalarGridSpec(
            num_scalar_prefetch=2,
            grid=(d // tn, p // tm),
            in_specs=[pl.BlockSpec((tm, f), lambda j, i, te, tv: (i, 0)),
                      pl.BlockSpec((None, None, f, tn), lambda j, i, te, tv: (layer, te[i], 0, j))],
            out_specs=pl.BlockSpec((tm, tn), lambda j, i, te, tv: (i, j)),
            scratch_shapes=[pltpu.VMEM((f, tn), BF16)]),
        out_shape=jax.ShapeDtypeStruct((p, d), F32),
        compiler_params=_params("arbitrary", "arbitrary"),
        name="moe_expert_down",
    )(tile_expert, tile_valid, act, w_down)


def _combine_kernel(pos_ref, x_ref, g_ref, w_ref, y_ref, o_ref, buf_ref, sem, *, tm):
    base = pl.program_id(0) * tm

    def row_copy(r, k):
        src = pos_ref[2 * (base + r) + k]
        return pltpu.make_async_copy(y_ref.at[pl.ds(src, 1), :], buf_ref.at[k, pl.ds(r, 1), :], sem)

    def start(r, carry):
        row_copy(r, 0).start()
        row_copy(r, 1).start()
        return carry

    def wait(r, carry):
        row_copy(r, 0).wait()
        row_copy(r, 1).wait()
        return carry

    lax.fori_loop(0, tm, start, 0)
    lax.fori_loop(0, tm, wait, 0)
    w = w_ref[...]
    y = w[:, 0:1] * buf_ref[0] + w[:, 1:2] * buf_ref[1]
    o_ref[...] = x_ref[...] + g_ref[...] * y


def moe_combine(x, g, wts, y, pos_flat, *, tm=256):
    t, d = x.shape
    tm = min(tm, t)
    return pl.pallas_call(
        functools.partial(_combine_kernel, tm=tm),
        grid_spec=pltpu.PrefetchScalarGridSpec(
            num_scalar_prefetch=1,
            grid=(t // tm,),
            in_specs=[pl.BlockSpec((tm, d), lambda i, pos: (i, 0)),
                      pl.BlockSpec((1, d), lambda i, pos: (0, 0)),
                      pl.BlockSpec((tm, 2), lambda i, pos: (i, 0)),
                      pl.BlockSpec(memory_space=pl.ANY)],
            out_specs=pl.BlockSpec((tm, d), lambda i, pos: (i, 0)),
            scratch_shapes=[pltpu.VMEM((2, tm, d), F32), pltpu.SemaphoreType.DMA(())]),
        out_shape=jax.ShapeDtypeStruct((t, d), F32),
        compiler_params=_params("arbitrary"),
        name="moe_combine",
    )(pos_flat, x, g, wts, y)


MOE_ROW_TILE = 256


def moe_ffn_residual(x, a, b, g, router_w, router_b, w_gate, w_up, w_down, layer):
    t, d = x.shape
    ne, tm = N_EXPERTS, MOE_ROW_TILE
    h, info, cnt = moe_router(x, a, b, router_w, router_b)
    counts = cnt[:, 0].astype(I32)
    padded = (counts + tm - 1) // tm * tm
    ends = jnp.cumsum(padded)
    starts = ends - padded
    n_tiles = 2 * t // tm + ne
    tile_row = jnp.arange(n_tiles, dtype=I32) * tm
    tile_expert = jnp.minimum(jnp.sum((tile_row[:, None] >= ends[None, :]).astype(I32), axis=1), ne - 1)
    tile_valid = (tile_row < ends[-1]).astype(I32)
    experts = info[0:2].astype(I32)
    pos = (starts[experts] + info[4:6].astype(I32)).T.reshape(-1)
    wts = info[2:4].T
    xs = scatter_rows(h, pos, n_tiles * tm)
    act = expert_up(xs, w_gate, w_up, layer, tile_expert, tile_valid, tm=tm)
    y = expert_down(act, w_down, layer, tile_expert, tile_valid, tm=tm)
    return moe_combine(x, g, wts, y, pos)


MOBA_W = MOBA_HEADS * HEAD_DIM
MLSTM_QK_W = MLSTM_HEADS * MLSTM_QK_DIM
MLSTM_V_W = MLSTM_HEADS * MLSTM_V_DIM
EVEN_MAIN_W = 3 * MOBA_W + 2 * MLSTM_QK_W + 2 * MLSTM_V_W
NSA_Q_W = NSA_HEADS * HEAD_DIM
NSA_KV_W = NSA_KV_HEADS * HEAD_DIM
NSA_GATE_W = 3 * NSA_HEADS
DIL_GROUP_W = DIL_HEADS * HEAD_DIM
DIL_W = len(DIL_PATTERNS) * DIL_GROUP_W
ODD_GATE_COL0 = NSA_Q_W + 6 * NSA_KV_W


def _pad_cols(w, width):
    return jnp.pad(w, ((0, 0), (0, width - w.shape[1])))


def moba_mlstm_mixer(x, a, b, g, w_in, w_out, qn_g, kn_g, i_b, f_b, out_g):
    w_main = w_in[:, :EVEN_MAIN_W].astype(BF16)
    w_gate = _pad_cols(w_in[:, EVEN_MAIN_W:], LANES).astype(BF16)
    z = norm_matmul(x, a, b, w_main)
    zg = norm_matmul(x, a, b, w_gate)
    kn, kmean = colprep(z, MOBA_W, MOBA_HEADS, kn_g, mean_rows=MOBA_BLOCK)
    vt = colprep(z, 2 * MOBA_W, MOBA_HEADS, transpose=True)
    o_a = moba_attention(z, 0, qn_g, kn, vt, kmean)
    c_q = 3 * MOBA_W
    c_k = c_q + MLSTM_QK_W
    c_v = c_k + MLSTM_QK_W
    c_o = c_v + MLSTM_V_W
    gate_bias = _pad_cols(jnp.concatenate([i_b, f_b]).reshape(1, -1), LANES)
    o_b = mlstm(z, c_q, c_k, c_v, c_o, zg, gate_bias, out_g)
    o = jnp.concatenate([o_a, o_b], axis=1)
    return matmul_residual(o, w_out.astype(BF16), x, g)


def nsa_dilated_mixer(x, a, b, g, w_in, w_out, nsa_qn_g, nsa_kn_g, pe_k, phik_w1, phik_b1, phik_w2,
                      pe_v, phiv_w1, phiv_b1, phiv_w2, dil_qn_g, dil_kn_g):
    t = x.shape[0]
    hd, kvw = HEAD_DIM, NSA_KV_W
    gc = ODD_GATE_COL0
    w_main = jnp.concatenate([w_in[:, :gc], w_in[:, gc + NSA_GATE_W:]], axis=1).astype(BF16)
    w_gate = _pad_cols(w_in[:, gc:gc + NSA_GATE_W], LANES).astype(BF16)
    z = norm_matmul(x, a, b, w_main)
    zg = norm_matmul(x, a, b, w_gate)
    r_heads = NSA_HEADS // NSA_KV_HEADS
    gates_t = zg[:, :NSA_GATE_W].reshape(t, NSA_KV_HEADS, r_heads, 3).transpose(1, 3, 2, 0)
    c_kc = NSA_Q_W

    def cmp_blocks(col0):
        s = NSA_CMP_STRIDE
        return (z[:, col0:col0 + kvw].reshape(t // s, s, NSA_KV_HEADS, hd)
                .transpose(2, 0, 1, 3).reshape(NSA_KV_HEADS, t // s, s * hd))

    kcmp = nsa_compress(cmp_blocks(c_kc), pe_k, phik_w1, phik_b1, phik_w2, nsa_kn_g, is_key=True)
    vcmp = nsa_compress(cmp_blocks(c_kc + kvw), pe_v, phiv_w1, phiv_b1, phiv_w2, nsa_kn_g, is_key=False)
    ksn = colprep(z, c_kc + 2 * kvw, NSA_KV_HEADS, nsa_kn_g)
    vs_t = colprep(z, c_kc + 3 * kvw, NSA_KV_HEADS, transpose=True)
    kwn = colprep(z, c_kc + 4 * kvw, NSA_KV_HEADS, nsa_kn_g)
    vw_t = colprep(z, c_kc + 5 * kvw, NSA_KV_HEADS, transpose=True)
    o_nsa = nsa_attention(z, 0, nsa_qn_g, gates_t, kcmp, vcmp, ksn, vs_t, kwn, vw_t)
    c_dq = gc
    n_dil = len(DIL_PATTERNS) * DIL_HEADS
    dqn = colprep(z, c_dq, n_dil, dil_qn_g, scale=hd ** -0.5)
    dkn = colprep(z, c_dq + DIL_W, n_dil, dil_kn_g)
    dvb = colprep(z, c_dq + 2 * DIL_W, n_dil)
    outs, lses = [], []
    for gi, (win, dil) in enumerate(DIL_PATTERNS):
        grp = lambda arr: arr[:, gi * DIL_GROUP_W:(gi + 1) * DIL_GROUP_W].reshape(t // dil, dil * DIL_GROUP_W)
        o_g, lse_g = dilated_group(grp(dqn), grp(dkn), grp(dvb), dil, back=win // dil)
        outs.append(o_g.reshape(t, DIL_GROUP_W))
        lses.append(lse_g.reshape(t, DIL_GROUP_W))
    o_dil = dilated_merge(outs, lses)
    o = jnp.concatenate([o_nsa, o_dil], axis=1)
    return matmul_residual(o, w_out.astype(BF16), x, g)


def kernel(x, c, ada_w, ada_b, norm_mix_g, norm_ffn_g, ev_w_in, ev_w_out, moba_qn_g, moba_kn_g, mlstm_i_b, mlstm_f_b, mlstm_out_g, od_w_in, od_w_out, nsa_qn_g, nsa_kn_g, nsa_pe_k, nsa_phik_w1, nsa_phik_b1, nsa_phik_w2, nsa_pe_v, nsa_phiv_w1, nsa_phiv_b1, nsa_phiv_w2, dil_qn_g, dil_kn_g, router_w, router_b, moe_w_gate, moe_w_up, moe_w_down):
    bsz, t, d = x.shape
    assert bsz == 1, "kernels are written for a single sequence"
    depth = ada_w.shape[0]
    mod = adaln(c, ada_w, ada_b)
    xs = x.reshape(t, d)
    for layer in range(depth):
        sh_m, sc_m, g_m, sh_f, sc_f, g_f = [m.reshape(1, d) for m in jnp.split(mod[layer], 6)]
        a_m = norm_mix_g[layer].reshape(1, d) * (1.0 + sc_m)
        a_f = norm_ffn_g[layer].reshape(1, d) * (1.0 + sc_f)
        j = layer // 2
        if layer % 2 == 0:
            xs = moba_mlstm_mixer(xs, a_m, sh_m, g_m, ev_w_in[j], ev_w_out[j], moba_qn_g[j], moba_kn_g[j],
                                  mlstm_i_b[j], mlstm_f_b[j], mlstm_out_g[j])
        else:
            xs = nsa_dilated_mixer(xs, a_m, sh_m, g_m, od_w_in[j], od_w_out[j], nsa_qn_g[j], nsa_kn_g[j],
                                   nsa_pe_k[j], nsa_phik_w1[j], nsa_phik_b1[j], nsa_phik_w2[j],
                                   nsa_pe_v[j], nsa_phiv_w1[j], nsa_phiv_b1[j], nsa_phiv_w2[j],
                                   dil_qn_g[j], dil_kn_g[j])
        xs = moe_ffn_residual(xs, a_f, sh_f, g_f, router_w, router_b, moe_w_gate, moe_w_up, moe_w_down, layer)
    return xs.reshape(bsz, t, d)
```

```python
import functools

import numpy as np
import jax
import jax.numpy as jnp
from jax import lax
from jax.experimental import pallas as pl
from jax.experimental.pallas import tpu as pltpu

F32 = jnp.float32
BF16 = jnp.bfloat16
I32 = jnp.int32

NORM_EPS = 1e-6
HEAD_DIM = 128
LANES = 128
VMEM_LIMIT_BYTES = 56 * 1024 * 1024
NEG = -1e30
LOG2E = 1.4426950408889634

MOBA_HEADS = 16
MOBA_BLOCK = 256
MOBA_TOPK = 3
MLSTM_HEADS = 4
MLSTM_QK_DIM = 256
MLSTM_V_DIM = 512
MLSTM_GATE_CAP = 15.0
NSA_HEADS = 16
NSA_KV_HEADS = 4
NSA_CMP_BLOCK = 32
NSA_CMP_STRIDE = 16
NSA_SLC_BLOCK = 64
NSA_SLC_TOPN = 16
NSA_WINDOW = 512
DIL_PATTERNS = ((128, 1), (512, 4), (2048, 16))
DIL_HEADS = 8
N_EXPERTS = 16
N_EXPERT_GROUPS = 4
EXPERTS_PER_GROUP = N_EXPERTS // N_EXPERT_GROUPS
D_FF_EXPERT = 1024

_NT = (((1,), (1,)), ((), ()))
_TN = (((0,), (0,)), ((), ()))


def _params(*sem):
    return pltpu.CompilerParams(dimension_semantics=sem, vmem_limit_bytes=VMEM_LIMIT_BYTES)


def _split_bf16(a):
    hi = a.astype(BF16)
    lo = (a - hi.astype(F32)).astype(BF16)
    return hi, lo


def _dot_nt_f32(a, b):
    a_hi, a_lo = _split_bf16(a)
    b_hi, b_lo = _split_bf16(b)
    d = functools.partial(lax.dot_general, dimension_numbers=_NT, preferred_element_type=F32)
    return d(a_hi, b_hi) + (d(a_hi, b_lo) + d(a_lo, b_hi))


def _rms(x, gain):
    return x * lax.rsqrt(jnp.mean(x * x, axis=-1, keepdims=True) + NORM_EPS) * gain


def _adaln_kernel(c_ref, w_ref, b_ref, o_ref):
    c = c_ref[...]
    ca = c * jax.nn.sigmoid(c)
    o_ref[...] = jnp.sum(w_ref[...] * ca, axis=0, keepdims=True) + b_ref[...]


def adaln(c, ada_w, ada_b):
    depth, d, n = ada_w.shape
    tn = 512
    out = pl.pallas_call(
        _adaln_kernel,
        grid=(depth, n // tn),
        in_specs=[pl.BlockSpec((d, 1), lambda l, j: (0, 0)),
                  pl.BlockSpec((None, d, tn), lambda l, j: (l, 0, j)),
                  pl.BlockSpec((None, 1, tn), lambda l, j: (l, 0, j))],
        out_specs=pl.BlockSpec((None, 1, tn), lambda l, j: (l, 0, j)),
        out_shape=jax.ShapeDtypeStruct((depth, 1, n), F32),
        compiler_params=_params("parallel", "parallel"),
        name="adaln",
    )(c.reshape(d, 1), ada_w, ada_b.reshape(depth, 1, n))
    return out.reshape(depth, n)


NORM_ROW_CHUNK = 128


def _norm_matmul_kernel(x_ref, a_ref, b_ref, w_ref, ws_ref, o_ref, os_ref, h_ref):
    @pl.when(pl.program_id(1) == 0)
    def _():
        rows = min(NORM_ROW_CHUNK, x_ref.shape[0])

        def chunk(c, carry):
            sl = pl.ds(pl.multiple_of(c * rows, rows), rows)
            h_ref[sl, :] = (_rms(x_ref[sl, :], a_ref[...]) + b_ref[...]).astype(BF16)
            return carry

        lax.fori_loop(0, x_ref.shape[0] // rows, chunk, 0)
        os_ref[...] = jnp.dot(h_ref[...], ws_ref[...], preferred_element_type=F32)

    o_ref[...] = jnp.dot(h_ref[...], w_ref[...], preferred_element_type=F32).astype(o_ref.dtype)


def norm_matmul(x, a, b, w, w_side, *, tm=1024, tn=512, out_dtype=F32):
    t, d = x.shape
    n = w.shape[1]
    ns = w_side.shape[1]
    tm, tn = min(tm, t), min(tn, n)
    return pl.pallas_call(
        _norm_matmul_kernel,
        grid=(t // tm, n // tn),
        in_specs=[pl.BlockSpec((tm, d), lambda i, j: (i, 0), pipeline_mode=pl.Buffered(1)),
                  pl.BlockSpec((1, d), lambda i, j: (0, 0)),
                  pl.BlockSpec((1, d), lambda i, j: (0, 0)),
                  pl.BlockSpec((d, tn), lambda i, j: (0, j)),
                  pl.BlockSpec((d, ns), lambda i, j: (0, 0))],
        out_specs=[pl.BlockSpec((tm, tn), lambda i, j: (i, j)),
                   pl.BlockSpec((tm, ns), lambda i, j: (i, 0))],
        out_shape=[jax.ShapeDtypeStruct((t, n), out_dtype), jax.ShapeDtypeStruct((t, ns), F32)],
        scratch_shapes=[pltpu.VMEM((tm, d), BF16)],
        compiler_params=_params("parallel", "arbitrary"),
        name="norm_matmul",
    )(x, a, b, w, w_side)


def _matmul_residual_kernel(a_ref, w_ref, x_ref, g_ref, o_ref):
    y = jnp.dot(a_ref[...], w_ref[...], preferred_element_type=F32)
    o_ref[...] = x_ref[...] + g_ref[...] * y


def matmul_residual(a, w, x, g, *, tm=1024, tn=512):
    t, k = a.shape
    d = w.shape[1]
    tm, tn = min(tm, t), min(tn, d)
    return pl.pallas_call(
        _matmul_residual_kernel,
        grid=(t // tm, d // tn),
        in_specs=[pl.BlockSpec((tm, k), lambda i, j: (i, 0)),
                  pl.BlockSpec((k, tn), lambda i, j: (0, j)),
                  pl.BlockSpec((tm, tn), lambda i, j: (i, j)),
                  pl.BlockSpec((1, tn), lambda i, j: (0, j))],
        out_specs=pl.BlockSpec((tm, tn), lambda i, j: (i, j)),
        out_shape=jax.ShapeDtypeStruct((t, d), F32),
        compiler_params=_params("parallel", "parallel"),
        name="matmul_residual",
    )(a, w, x, g)


COLPREP_HEADS = 4


def _colprep_kernel(z_ref, g_ref, *o_refs, norm, scale, transpose, mean_rows):
    hd = HEAD_DIM
    for h in range(z_ref.shape[1] // hd):
        cols = slice(h * hd, (h + 1) * hd)
        y = z_ref[:, cols]
        if norm:
            y = _rms(y, g_ref[...])
        if mean_rows:
            tm = y.shape[0]
            o_refs[1][:, cols] = jnp.mean(y.reshape(tm // mean_rows, mean_rows, hd), axis=1)
        if scale != 1.0:
            y = y * scale
        if transpose:
            o_refs[0][cols, :] = y.T.astype(o_refs[0].dtype)
        else:
            o_refs[0][:, cols] = y.astype(o_refs[0].dtype)


def colprep(z, col0, nheads, gain=None, *, scale=1.0, transpose=False, mean_rows=0, tm=2048,
            out_dtype=BF16):
    t = z.shape[0]
    tm = min(tm, t)
    hb = COLPREP_HEADS
    bw = hb * HEAD_DIM
    assert nheads % hb == 0 and col0 % bw == 0
    c0 = col0 // bw
    norm = gain is not None
    g = (gain if norm else jnp.ones((HEAD_DIM,), F32)).reshape(1, HEAD_DIM)
    w = nheads * HEAD_DIM
    if transpose:
        out_shape = [jax.ShapeDtypeStruct((w, t), out_dtype)]
        out_specs = [pl.BlockSpec((bw, tm), lambda i, h: (h, i))]
    else:
        out_shape = [jax.ShapeDtypeStruct((t, w), out_dtype)]
        out_specs = [pl.BlockSpec((tm, bw), lambda i, h: (i, h))]
    if mean_rows:
        out_shape.append(jax.ShapeDtypeStruct((t // mean_rows, w), F32))
        out_specs.append(pl.BlockSpec((tm // mean_rows, bw), lambda i, h: (i, h)))
    outs = pl.pallas_call(
        functools.partial(_colprep_kernel, norm=norm, scale=scale, transpose=transpose,
                          mean_rows=mean_rows),
        grid=(t // tm, nheads // hb),
        in_specs=[pl.BlockSpec((tm, bw), lambda i, h: (i, c0 + h)),
                  pl.BlockSpec((1, HEAD_DIM), lambda i, h: (0, 0))],
        out_specs=out_specs,
        out_shape=out_shape,
        compiler_params=_params("parallel", "parallel"),
        name="colprep",
    )(z, g)
    return outs if mean_rows else outs[0]


ACC_ROWS = HEAD_DIM + 16


def _with_ones_rows(vt):
    return jnp.concatenate([vt, jnp.ones((ACC_ROWS - HEAD_DIM, vt.shape[1]), vt.dtype)], axis=0)


def _flash_step(s_chunks, vt1, m, acc_ref, slot):
    m_new = m
    for s in s_chunks:
        m_new = jnp.maximum(m_new, jnp.max(s, axis=0, keepdims=True))
    alpha = jnp.exp2(m - m_new)
    ps = [jnp.exp2((s - m_new).astype(BF16)) for s in s_chunks]
    p16 = jnp.concatenate(ps, axis=0) if len(ps) > 1 else ps[0]
    acc_ref[slot] = alpha * acc_ref[slot] + jnp.dot(vt1, p16, preferred_element_type=F32)
    return m_new


def _moba_kernel(zq_ref, g_ref, kn_ref, vt_ref, km_ref, o_ref, bias_ref, acc_ref, *, tq, nb, topk):
    i = pl.program_id(1)
    qn = _rms(zq_ref[...], g_ref[...])
    qs = (qn * (HEAD_DIM ** -0.5 * LOG2E)).astype(BF16)
    pos = i * tq + lax.broadcasted_iota(I32, (1, tq), 1)
    cur = (pos // MOBA_BLOCK).astype(F32)
    bidx = lax.broadcasted_iota(I32, (nb, tq), 0).astype(F32)

    gate = jnp.where(bidx < cur, _dot_nt_f32(km_ref[...], qn), -jnp.inf)
    sel = jnp.zeros((nb, tq), F32)
    for _ in range(topk):
        mx = jnp.max(gate, axis=0, keepdims=True)
        is_max = (gate == mx) & (gate > -jnp.inf)
        first = jnp.min(jnp.where(is_max, bidx, float(nb)), axis=0, keepdims=True)
        pick = bidx == first
        sel = jnp.where(pick, 1.0, sel)
        gate = jnp.where(pick, -jnp.inf, gate)
    sel = jnp.where(bidx == cur, 1.0, sel)
    bias_ref[...] = jnp.where(sel > 0.5, 0.0, NEG)
    acc_ref[...] = jnp.zeros_like(acc_ref)
    nblk = tq // MOBA_BLOCK

    def tile(off, blk0, causal, m):
        chunks = []
        for b in range(nblk):
            k = kn_ref[pl.ds(off + b * MOBA_BLOCK, MOBA_BLOCK), :]
            s = lax.dot_general(k, qs, _NT, preferred_element_type=F32) + bias_ref[pl.ds(blk0 + b, 1), :]
            if causal:
                kpos = off + b * MOBA_BLOCK + lax.broadcasted_iota(I32, (MOBA_BLOCK, 1), 0)
                s = jnp.where(kpos <= pos, s, NEG)
            chunks.append(s)
        return _flash_step(chunks, _with_ones_rows(vt_ref[:, pl.ds(off, tq)]), m, acc_ref, 0)

    def past(j, m):
        return tile(pl.multiple_of(j * tq, tq), j * nblk, False, m)

    def pair(jj, m):
        j = odd + 2 * jj
        return past(j + 1, past(j, m))

    odd = i % 2
    m = jnp.full((1, tq), NEG, F32)
    m = lax.cond(odd == 1, lambda mm: past(0, mm), lambda mm: mm, m)
    m = lax.fori_loop(0, i // 2, pair, m)
    tile(pl.multiple_of(i * tq, tq), i * nblk, True, m)
    acc = acc_ref[0]
    o_ref[...] = (acc[:HEAD_DIM] / acc[HEAD_DIM:HEAD_DIM + 1]).T.astype(o_ref.dtype)


def moba_attention(z, q_col0, q_gain, kn, vt, kmean, *, tq=1024):
    t = z.shape[0]
    tq = min(tq, t)
    nb = t // MOBA_BLOCK
    h = MOBA_HEADS
    c0 = q_col0 // HEAD_DIM
    return pl.pallas_call(
        functools.partial(_moba_kernel, tq=tq, nb=nb, topk=min(MOBA_TOPK, nb)),
        grid=(h, t // tq),
        in_specs=[pl.BlockSpec((tq, HEAD_DIM), lambda hh, i: (i, c0 + hh)),
                  pl.BlockSpec((1, HEAD_DIM), lambda hh, i: (0, 0)),
                  pl.BlockSpec((t, HEAD_DIM), lambda hh, i: (0, hh)),
                  pl.BlockSpec((HEAD_DIM, t), lambda hh, i: (hh, 0)),
                  pl.BlockSpec((nb, HEAD_DIM), lambda hh, i: (0, hh))],
        out_specs=pl.BlockSpec((tq, HEAD_DIM), lambda hh, i: (i, hh)),
        out_shape=jax.ShapeDtypeStruct((t, h * HEAD_DIM), BF16),
        scratch_shapes=[pltpu.VMEM((nb, tq), F32), pltpu.VMEM((1, ACC_ROWS, tq), F32)],
        compiler_params=_params("parallel", "parallel"),
        name="moba",
    )(z, q_gain.reshape(1, HEAD_DIM), kn, vt, kmean)


def _mlstm_kernel(zq_ref, zk_ref, zv_ref, zo_ref, zg_ref, gb_ref, og_ref, o_ref, ct_ref, n_ref, m_ref,
                  *, chunk):
    dk, dv, cap = MLSTM_QK_DIM, MLSTM_V_DIM, MLSTM_GATE_CAP

    @pl.when(pl.program_id(0) == 0)
    def _():
        ct_ref[...] = jnp.zeros_like(ct_ref)
        n_ref[...] = jnp.zeros_like(n_ref)
        m_ref[...] = jnp.zeros_like(m_ref)

    pre = cap * jnp.tanh((zg_ref[...] + gb_ref[...]) / cap)
    log_f = jnp.minimum(pre, 0.0) - jnp.log1p(jnp.exp(-jnp.abs(pre)))
    row = lax.broadcasted_iota(I32, (chunk, chunk), 0)
    col = lax.broadcasted_iota(I32, (chunk, chunk), 1)
    causal = col <= row
    tri = causal.astype(BF16)
    p1 = log_f.astype(BF16)
    r1 = log_f - p1.astype(F32)
    p2 = r1.astype(BF16)
    p3 = (r1 - p2.astype(F32)).astype(BF16)
    mm = functools.partial(jnp.dot, preferred_element_type=F32)
    b = mm(tri, p1) + (mm(tri, p2) + mm(tri, p3))
    pre_t = pre.T
    b_t = b.T

    for h in range(MLSTM_HEADS):
        li_row = pre_t[h:h + 1, :]
        li_col = pre[:, h:h + 1]
        b_row = b_t[MLSTM_HEADS + h:MLSTM_HEADS + h + 1, :]
        b_col = b[:, MLSTM_HEADS + h:MLSTM_HEADS + h + 1]
        m_prev = m_ref[h][:, :1]
        log_d = jnp.where(causal, b_col - b_row + li_row, -jnp.inf)
        log_inter = b_col + m_prev
        m_s = jnp.maximum(log_inter, jnp.max(log_d, axis=1, keepdims=True))
        dmat = jnp.exp(log_d - m_s)
        inter_w = jnp.exp(log_inter - m_s)
        qf = zq_ref[:, h * dk:(h + 1) * dk] * dk ** -0.5
        kf = zk_ref[:, h * dk:(h + 1) * dk]
        q16 = qf.astype(BF16)
        v16 = zv_ref[:, h * dv:(h + 1) * dv].astype(BF16)
        s = lax.dot_general(q16, kf.astype(BF16), _NT, preferred_element_type=F32) * dmat
        ct = ct_ref[h]
        num = mm(s.astype(BF16), v16) + inter_w * mm(q16, ct.astype(BF16))
        den = (jnp.sum(s, axis=1, keepdims=True)
               + inter_w * jnp.sum(qf * n_ref[h], axis=1, keepdims=True))
        hh = num / jnp.maximum(jnp.abs(den), jnp.exp(-m_s))
        hn = _rms(hh, og_ref[:, h * dv:(h + 1) * dv])
        o_ref[:, h * dv:(h + 1) * dv] = (hn * jax.nn.sigmoid(zo_ref[:, h * dv:(h + 1) * dv])).astype(o_ref.dtype)
        b_end = b_col[chunk - 1:chunk, :]
        log_w = b_end - b_col + li_col
        m_new = jnp.maximum(b_end + m_prev, jnp.max(log_w, axis=0, keepdims=True))
        kw = kf * jnp.exp(log_w - m_new)
        decay = jnp.exp(b_end + m_prev - m_new)
        ct_ref[h] = decay * ct + lax.dot_general(kw.astype(BF16), v16, _TN, preferred_element_type=F32)
        n_ref[h] = decay * n_ref[h] + jnp.sum(kw, axis=0, keepdims=True)
        m_ref[h] = jnp.broadcast_to(m_new, (1, LANES))


def mlstm(z, q_col0, k_col0, v_col0, o_col0, zg, gate_bias, out_gain, *, chunk=256):
    t = z.shape[0]
    chunk = min(chunk, t)
    wk, wv = MLSTM_HEADS * MLSTM_QK_DIM, MLSTM_HEADS * MLSTM_V_DIM
    return pl.pallas_call(
        functools.partial(_mlstm_kernel, chunk=chunk),
        grid=(t // chunk,),
        in_specs=[pl.BlockSpec((chunk, wk), lambda c: (c, q_col0 // wk)),
                  pl.BlockSpec((chunk, wk), lambda c: (c, k_col0 // wk)),
                  pl.BlockSpec((chunk, wv), lambda c: (c, v_col0 // wv)),
                  pl.BlockSpec((chunk, wv), lambda c: (c, o_col0 // wv)),
                  pl.BlockSpec((chunk, LANES), lambda c: (c, 0)),
                  pl.BlockSpec((1, LANES), lambda c: (0, 0)),
                  pl.BlockSpec((1, wv), lambda c: (0, 0))],
        out_specs=pl.BlockSpec((chunk, wv), lambda c: (c, 0)),
        out_shape=jax.ShapeDtypeStruct((t, wv), BF16),
        scratch_shapes=[pltpu.VMEM((MLSTM_HEADS, MLSTM_QK_DIM, MLSTM_V_DIM), F32),
                        pltpu.VMEM((MLSTM_HEADS, 1, MLSTM_QK_DIM), F32),
                        pltpu.VMEM((MLSTM_HEADS, 1, LANES), F32)],
        compiler_params=_params("arbitrary"),
        name="mlstm",
    )(z, z, z, z, zg, gate_bias, out_gain.reshape(1, wv))


def _nsa_compress_kernel(a_ref, pe_ref, w1_ref, b1_ref, w2_ref, g_ref, o_ref, *, is_key):
    n, half = a_ref.shape
    w1 = w1_ref[...]
    bias = jnp.sum(w1 * pe_ref[...], axis=0, keepdims=True) + b1_ref[...]
    w1 = w1.astype(BF16)
    a16 = a_ref[...].astype(BF16)
    u = jnp.dot(a16, w1[:half], preferred_element_type=F32)
    v = jnp.dot(a16, w1[half:], preferred_element_type=F32)
    x = u + pltpu.roll(v, n - 1, 0) + bias
    hid = 0.5 * x * (1.0 + jnp.tanh(np.sqrt(2.0 / np.pi) * (x + 0.044715 * (x * x * x))))
    y = jnp.dot(hid.astype(BF16), w2_ref[...].astype(BF16), preferred_element_type=F32)
    if is_key:
        y = _rms(y, g_ref[...])
    o_ref[...] = y.astype(o_ref.dtype)


def nsa_compress(a, pe, w1, b1, w2, gain, *, is_key):
    g, n, half = a.shape
    hd = HEAD_DIM
    out_shape, out_spec = jax.ShapeDtypeStruct((g, n, hd), BF16), pl.BlockSpec((None, n, hd), lambda i: (i, 0, 0))
    full = lambda shape: pl.BlockSpec(shape, lambda i: (0,) * len(shape))
    return pl.pallas_call(
        functools.partial(_nsa_compress_kernel, is_key=is_key),
        grid=(g,),
        in_specs=[pl.BlockSpec((None, n, half), lambda i: (i, 0, 0)),
                  full((2 * half, 1)), full((2 * half, hd)), full((1, hd)), full((hd, hd)), full((1, hd))],
        out_specs=out_spec,
        out_shape=out_shape,
        compiler_params=_params("parallel"),
        name="nsa_compress",
    )(a, pe.reshape(2 * half, 1), w1, b1.reshape(1, hd), w2, gain.reshape(1, hd))


NSA_SLC_KEY_TILE = 1024
NSA_SLC_SUB_KEYS = 256


def _nsa_kernel(zq_ref, qg_ref, gate_ref, kc_ref, vc_ref, mt_ref, ks_ref, vst_ref, kw_ref, vwt_ref,
                o_ref, bias_ref, acc_ref, *, tq, nc, ns, ntop, kt):
    i = pl.program_id(1)
    r_heads = NSA_HEADS // NSA_KV_HEADS
    w = r_heads * tq
    hd = HEAD_DIM
    scale = hd ** -0.5 * LOG2E
    qs = jnp.concatenate(
        [(_rms(zq_ref[:, r * hd:(r + 1) * hd], qg_ref[...]) * scale).astype(BF16) for r in range(r_heads)],
        axis=0)
    pos_t = i * tq + lax.broadcasted_iota(I32, (1, tq), 1)
    pos = i * tq + (lax.broadcasted_iota(I32, (1, w), 1) & (tq - 1))

    s = lax.dot_general(kc_ref[...], qs, _NT, preferred_element_type=F32)
    cmp_end = lax.broadcasted_iota(I32, (nc, 1), 0) * NSA_CMP_STRIDE + (NSA_CMP_BLOCK - 1)
    cmask = cmp_end <= pos
    s = jnp.where(cmask, s, NEG)
    p = jnp.where(cmask, jnp.exp2(s - jnp.max(s, axis=0, keepdims=True)), 0.0)
    den = jnp.sum(p, axis=0, keepdims=True)
    p = p * (1.0 / jnp.where(den > 0.0, den, 1.0))
    o_cmp = lax.dot_general(vc_ref[...], p.astype(BF16), _TN, preferred_element_type=F32)
    imp = p[:, 0:tq]
    for r in range(1, r_heads):
        imp = imp + p[:, r * tq:(r + 1) * tq]
    imp_hi, imp_lo = _split_bf16(imp)
    p_slc = (jnp.dot(mt_ref[...], imp_hi, preferred_element_type=F32)
             + jnp.dot(mt_ref[...], imp_lo, preferred_element_type=F32))

    blk = lax.broadcasted_iota(I32, (ns, tq), 0).astype(F32)
    cur = (pos_t // NSA_SLC_BLOCK).astype(F32)
    forced = (blk == 0.0) | (blk == cur) | (blk == cur - 1.0)
    score0 = jnp.where(blk > cur, -jnp.inf, jnp.where(forced, jnp.inf, p_slc))

    def pick_one(_, carry):
        score, sel = carry
        mx = jnp.max(score, axis=0, keepdims=True)
        is_max = (score == mx) & (score > -jnp.inf)
        first = jnp.min(jnp.where(is_max, blk, float(ns)), axis=0, keepdims=True)
        pick = blk == first
        return jnp.where(pick, -jnp.inf, score), jnp.where(pick, 1.0, sel)

    _, sel = lax.fori_loop(0, ntop, pick_one, (score0, jnp.zeros((ns, tq), F32)))
    bias = jnp.where(sel > 0.5, 0.0, NEG)
    bias_ref[...] = jnp.concatenate([bias] * r_heads, axis=1)
    acc_ref[...] = jnp.zeros_like(acc_ref)

    sub = NSA_SLC_SUB_KEYS
    per_sub = sub // NSA_SLC_BLOCK

    def slc_tile(off, blk0, causal, m):
        chunks = []
        for c in range(kt // sub):
            s = lax.dot_general(ks_ref[pl.ds(off + c * sub, sub), :], qs, _NT, preferred_element_type=F32)
            for b in range(per_sub):
                r = c * per_sub + b
                sb = s[b * NSA_SLC_BLOCK:(b + 1) * NSA_SLC_BLOCK, :] + bias_ref[pl.ds(blk0 + r, 1), :]
                if causal:
                    kpos = off + r * NSA_SLC_BLOCK + lax.broadcasted_iota(I32, (NSA_SLC_BLOCK, 1), 0)
                    sb = jnp.where(kpos <= pos, sb, NEG)
                chunks.append(sb)
        return _flash_step(chunks, _with_ones_rows(vst_ref[:, pl.ds(off, kt)]), m, acc_ref, 0)

    def slc_past(j, m):
        return slc_tile(pl.multiple_of(j * kt, kt), j * (kt // NSA_SLC_BLOCK), False, m)

    def slc_pair(jj, m):
        j = odd + 2 * jj
        return slc_past(j + 1, slc_past(j, m))

    n_past = (i * tq) // kt
    odd = n_past % 2
    m = jnp.full((1, w), NEG, F32)
    m = lax.cond(odd == 1, lambda mm: slc_past(0, mm), lambda mm: mm, m)
    m = lax.fori_loop(0, n_past // 2, slc_pair, m)
    slc_tile(pl.multiple_of(n_past * kt, kt), n_past * (kt // NSA_SLC_BLOCK), True, m)

    wk = NSA_WINDOW + tq
    off = pl.multiple_of(jnp.maximum(i * tq - NSA_WINDOW, 0), tq)
    s = lax.dot_general(kw_ref[pl.ds(off, wk), :], qs, _NT, preferred_element_type=F32)
    kpos = off + lax.broadcasted_iota(I32, (wk, 1), 0)
    s = jnp.where((kpos <= pos) & (kpos > pos - NSA_WINDOW), s, NEG)
    p = jnp.exp2(s - jnp.max(s, axis=0, keepdims=True))
    o_win = (jnp.dot(vwt_ref[:, pl.ds(off, wk)], p.astype(BF16), preferred_element_type=F32)
             / jnp.sum(p, axis=0, keepdims=True))

    gates = jax.nn.sigmoid(gate_ref[...])
    acc = acc_ref[0]
    o_slc = acc[:hd] / acc[hd:hd + 1]
    for r in range(r_heads):
        cols = slice(r * tq, (r + 1) * tq)
        o = (gates[0, r:r + 1, :] * o_cmp[:, cols] + gates[1, r:r + 1, :] * o_slc[:, cols]
             + gates[2, r:r + 1, :] * o_win[:, cols])
        o_ref[:, r * hd:(r + 1) * hd] = o.T.astype(o_ref.dtype)


def nsa_slc_weights(ns, nc):
    ratio = NSA_SLC_BLOCK // NSA_CMP_STRIDE
    m = np.zeros((ns, nc), np.float32)
    for j in range(ns):
        for d, wgt in [(-1, 0.5)] + [(k, 1.0) for k in range(ratio - 1)] + [(ratio - 1, 0.5)]:
            n = ratio * j + d
            if 0 <= n < nc - 1:
                m[j, n] = wgt
    return m


def nsa_attention(z, q_col0, q_gain, gates_t, kcmp, vcmp, ksn, vs_t, kwn, vw_t, *, tq=256):
    t = z.shape[0]
    g, r_heads, hd = NSA_KV_HEADS, NSA_HEADS // NSA_KV_HEADS, HEAD_DIM
    nc, ns = t // NSA_CMP_STRIDE, t // NSA_SLC_BLOCK
    mt = jnp.asarray(nsa_slc_weights(ns, nc), BF16)
    qw = r_heads * hd
    once = pl.Buffered(1)
    return pl.pallas_call(
        functools.partial(_nsa_kernel, tq=tq, nc=nc, ns=ns, ntop=min(NSA_SLC_TOPN, ns),
                          kt=min(NSA_SLC_KEY_TILE, t)),
        grid=(g, t // tq),
        in_specs=[pl.BlockSpec((tq, qw), lambda gg, i: (i, q_col0 // qw + gg)),
                  pl.BlockSpec((1, hd), lambda gg, i: (0, 0)),
                  pl.BlockSpec((None, 3, r_heads, tq), lambda gg, i: (gg, 0, 0, i)),
                  pl.BlockSpec((None, nc, hd), lambda gg, i: (gg, 0, 0)),
                  pl.BlockSpec((None, nc, hd), lambda gg, i: (gg, 0, 0)),
                  pl.BlockSpec((ns, nc), lambda gg, i: (0, 0)),
                  pl.BlockSpec((t, hd), lambda gg, i: (0, gg), pipeline_mode=once),
                  pl.BlockSpec((hd, t), lambda gg, i: (gg, 0), pipeline_mode=once),
                  pl.BlockSpec((t, hd), lambda gg, i: (0, gg), pipeline_mode=once),
                  pl.BlockSpec((hd, t), lambda gg, i: (gg, 0), pipeline_mode=once)],
        out_specs=pl.BlockSpec((tq, qw), lambda gg, i: (i, gg)),
        out_shape=jax.ShapeDtypeStruct((t, NSA_HEADS * hd), BF16),
        scratch_shapes=[pltpu.VMEM((ns, r_heads * tq), F32), pltpu.VMEM((1, ACC_ROWS, r_heads * tq), F32)],
        compiler_params=_params("parallel", "arbitrary"),
        name="nsa",
    )(z, q_gain.reshape(1, hd), gates_t, kcmp, vcmp, mt, ksn, vs_t, kwn, vw_t)


def _dilated_kernel(q_ref, kp_ref, kc_ref, vp_ref, vc_ref, o_ref, lse_ref, *, tq, back):
    i = pl.program_id(1)
    hd = HEAD_DIM
    qpos = i * tq + lax.broadcasted_iota(I32, (tq, 1), 0)
    kpos = (i - 1) * tq + lax.broadcasted_iota(I32, (1, 2 * tq), 1)
    mask = (kpos <= qpos) & (kpos >= qpos - back) & (kpos >= 0)
    for j in range(DIL_HEADS):
        cols = slice(j * hd, (j + 1) * hd)
        k = jnp.concatenate([kp_ref[:, cols], kc_ref[:, cols]], axis=0)
        v = jnp.concatenate([vp_ref[:, cols], vc_ref[:, cols]], axis=0)
        s = lax.dot_general(q_ref[:, cols], k, _NT, preferred_element_type=F32)
        s = jnp.where(mask, s, NEG)
        m = jnp.max(s, axis=1, keepdims=True)
        p = jnp.exp(s - m)
        l = jnp.sum(p, axis=1, keepdims=True)
        o_ref[:, cols] = jnp.dot(p.astype(BF16), v, preferred_element_type=F32) / l
        lse_ref[:, cols] = jnp.broadcast_to(m + jnp.log(l), (tq, hd))


def dilated_group(qd, kd, vd, dil, *, back, tq=128):
    td, wd = qd.shape
    hw = wd // dil
    tq = min(tq, td)
    cur = pl.BlockSpec((tq, hw), lambda r, i: (i, r))
    prev = pl.BlockSpec((tq, hw), lambda r, i: (jnp.maximum(i - 1, 0), r))
    return pl.pallas_call(
        functools.partial(_dilated_kernel, tq=tq, back=back),
        grid=(dil, td // tq),
        in_specs=[cur, prev, cur, prev, cur],
        out_specs=[cur, cur],
        out_shape=[jax.ShapeDtypeStruct((td, wd), F32), jax.ShapeDtypeStruct((td, wd), F32)],
        compiler_params=_params("parallel", "parallel"),
        name="dilated",
    )(qd, kd, kd, vd, vd)


def _dilated_merge_kernel(*refs):
    n = (len(refs) - 1) // 2
    o_refs, lse_refs, out_ref = refs[:n], refs[n:2 * n], refs[2 * n]
    lse = [r[...] for r in lse_refs]
    m = functools.reduce(jnp.maximum, lse)
    e = [jnp.exp(x - m) for x in lse]
    tot = functools.reduce(lambda a, b: a + b, e)
    acc = sum(ei * r[...] for ei, r in zip(e, o_refs))
    out_ref[...] = (acc / tot).astype(out_ref.dtype)


def dilated_merge(outs, lses, *, tm=512):
    t, wd = outs[0].shape
    tm = min(tm, t)
    spec = pl.BlockSpec((tm, wd), lambda i: (i, 0))
    return pl.pallas_call(
        _dilated_merge_kernel,
        grid=(t // tm,),
        in_specs=[spec] * (2 * len(outs)),
        out_specs=spec,
        out_shape=jax.ShapeDtypeStruct((t, wd), BF16),
        compiler_params=_params("parallel"),
        name="dilated_merge",
    )(*outs, *lses)


def _top2_of_4(v):
    best, loc0 = v[0], jnp.zeros_like(v[0])
    for j in range(1, 4):
        better = v[j] > best
        best = jnp.where(better, v[j], best)
        loc0 = jnp.where(better, float(j), loc0)
    best1, loc1 = jnp.full_like(v[0], -jnp.inf), jnp.zeros_like(v[0])
    for j in range(4):
        better = (v[j] > best1) & (loc0 != float(j))
        best1 = jnp.where(better, v[j], best1)
        loc1 = jnp.where(better, float(j), loc1)
    return loc0, loc1


def _router_kernel(x_ref, a_ref, b_ref, wt_ref, rb_ref, h_ref, info_ref, cnt_ref, carry_ref, *, tm):
    ne, ng, eg = N_EXPERTS, N_EXPERT_GROUPS, EXPERTS_PER_GROUP

    @pl.when(pl.program_id(0) == 0)
    def _():
        carry_ref[...] = jnp.zeros_like(carry_ref)

    h = _rms(x_ref[...], a_ref[...]) + b_ref[...]
    h_ref[...] = h
    s = jax.nn.sigmoid(_dot_nt_f32(wt_ref[...], h))
    sel = s + rb_ref[...]
    srow = [s[e:e + 1, :] for e in range(ne)]
    selrow = [sel[e:e + 1, :] for e in range(ne)]
    gscore = []
    for g in range(ng):
        a0, a1, a2, a3 = selrow[eg * g:eg * g + 4]
        hi1, lo1, hi2, lo2 = jnp.maximum(a0, a1), jnp.minimum(a0, a1), jnp.maximum(a2, a3), jnp.minimum(a2, a3)
        gscore.append(jnp.maximum(hi1, hi2) + jnp.maximum(jnp.minimum(hi1, hi2), jnp.maximum(lo1, lo2)))
    best, grp = gscore[0], jnp.zeros_like(gscore[0])
    for g in range(1, ng):
        better = gscore[g] > best
        best = jnp.where(better, gscore[g], best)
        grp = jnp.where(better, float(g), grp)
    pick = lambda rows, j: sum(jnp.where(grp == float(g), rows[eg * g + j], 0.0) for g in range(ng))
    loc0, loc1 = _top2_of_4([pick(selrow, j) for j in range(eg)])
    s_in = [pick(srow, j) for j in range(eg)]
    s0 = sum(jnp.where(loc0 == float(j), s_in[j], 0.0) for j in range(eg))
    s1 = sum(jnp.where(loc1 == float(j), s_in[j], 0.0) for j in range(eg))
    e0 = grp * eg + loc0
    e1 = grp * eg + loc1
    tot = s0 + s1
    eidx = lax.broadcasted_iota(I32, (ne, tm), 0).astype(F32)
    oh0 = eidx == e0
    oh1 = eidx == e1
    onehot = (oh0 | oh1).astype(BF16)
    upper = (lax.broadcasted_iota(I32, (tm, tm), 0) < lax.broadcasted_iota(I32, (tm, tm), 1)).astype(BF16)
    before = jnp.dot(onehot, upper, preferred_element_type=F32) + carry_ref[:, :1]
    r0 = jnp.sum(jnp.where(oh0, before, 0.0), axis=0, keepdims=True)
    r1 = jnp.sum(jnp.where(oh1, before, 0.0), axis=0, keepdims=True)
    carry_ref[...] = carry_ref[...] + jnp.sum(onehot.astype(F32), axis=1, keepdims=True)
    cnt_ref[...] = carry_ref[...]
    rid = lax.broadcasted_iota(I32, (8, tm), 0)
    rows = (e0, e1, s0 / tot, s1 / tot, r0, r1)
    info = jnp.zeros((8, tm), F32)
    for k, r in enumerate(rows):
        info = jnp.where(rid == k, r, info)
    info_ref[...] = info


def moe_router(x, a, b, router_w, router_b, *, tm=512):
    t, d = x.shape
    tm = min(tm, t)
    ne = N_EXPERTS
    return pl.pallas_call(
        functools.partial(_router_kernel, tm=tm),
        grid=(t // tm,),
        in_specs=[pl.BlockSpec((tm, d), lambda i: (i, 0)),
                  pl.BlockSpec((1, d), lambda i: (0, 0)),
                  pl.BlockSpec((1, d), lambda i: (0, 0)),
                  pl.BlockSpec((ne, d), lambda i: (0, 0)),
                  pl.BlockSpec((ne, 1), lambda i: (0, 0))],
        out_specs=[pl.BlockSpec((tm, d), lambda i: (i, 0)),
                   pl.BlockSpec((8, tm), lambda i: (0, i)),
                   pl.BlockSpec((ne, LANES), lambda i: (0, 0))],
        out_shape=[jax.ShapeDtypeStruct((t, d), F32),
                   jax.ShapeDtypeStruct((8, t), F32),
                   jax.ShapeDtypeStruct((ne, LANES), F32)],
        scratch_shapes=[pltpu.VMEM((ne, LANES), F32)],
        compiler_params=_params("arbitrary"),
        name="moe_router",
    )(x, a, b, router_w.T, router_b.reshape(ne, 1))


def _scatter_rows_kernel(pos_ref, h_ref, init_ref, o_ref, sem, *, tm):
    del init_ref
    base = pl.program_id(0) * tm

    def row_copy(r, k):
        dst = pos_ref[2 * (base + r) + k]
        return pltpu.make_async_copy(h_ref.at[pl.ds(r, 1), :], o_ref.at[pl.ds(dst, 1), :], sem)

    def start(r, carry):
        row_copy(r, 0).start()
        row_copy(r, 1).start()
        return carry

    def wait(r, carry):
        row_copy(r, 0).wait()
        row_copy(r, 1).wait()
        return carry

    lax.fori_loop(0, tm, start, 0)
    lax.fori_loop(0, tm, wait, 0)


def scatter_rows(h, pos_flat, n_rows, *, tm=256):
    t, d = h.shape
    tm = min(tm, t)
    return pl.pallas_call(
        functools.partial(_scatter_rows_kernel, tm=tm),
        grid_spec=pltpu.PrefetchScalarGridSpec(
            num_scalar_prefetch=1,
            grid=(t // tm,),
            in_specs=[pl.BlockSpec((tm, d), lambda i, pos: (i, 0)),
                      pl.BlockSpec(memory_space=pl.ANY)],
            out_specs=pl.BlockSpec(memory_space=pl.ANY),
            scratch_shapes=[pltpu.SemaphoreType.DMA(())]),
        out_shape=jax.ShapeDtypeStruct((n_rows, d), h.dtype),
        input_output_aliases={2: 0},
        compiler_params=_params("arbitrary"),
        name="moe_scatter",
    )(pos_flat, h, jnp.zeros((n_rows, d), h.dtype))


def _expert_changed(te_ref, i):
    return (i == 0) | (te_ref[i] != te_ref[jnp.maximum(i - 1, 0)])


def _expert_up_kernel(te_ref, tv_ref, xs_ref, wg_ref, wu_ref, o_ref, wg16_ref, wu16_ref):
    i = pl.program_id(1)

    @pl.when(_expert_changed(te_ref, i))
    def _():
        wg16_ref[...] = wg_ref[...].astype(BF16)
        wu16_ref[...] = wu_ref[...].astype(BF16)

    @pl.when(tv_ref[i] > 0)
    def _():
        x16 = xs_ref[...].astype(BF16)
        g = jnp.dot(x16, wg16_ref[...], preferred_element_type=F32)
        u = jnp.dot(x16, wu16_ref[...], preferred_element_type=F32)
        o_ref[...] = (g * jax.nn.sigmoid(g) * u).astype(o_ref.dtype)

    @pl.when(tv_ref[i] == 0)
    def _():
        o_ref[...] = jnp.zeros_like(o_ref)


def expert_up(xs, w_gate, w_up, layer, tile_expert, tile_valid, *, tm, tf=512):
    p, d = xs.shape
    f = w_gate.shape[3]
    tf = min(tf, f)
    w_spec = pl.BlockSpec((None, None, d, tf), lambda j, i, te, tv: (layer, te[i], 0, j),
                          pipeline_mode=pl.Buffered(1))
    return pl.pallas_call(
        _expert_up_kernel,
        grid_spec=pltpu.PrefetchScalarGridSpec(
            num_scalar_prefetch=2,
            grid=(f // tf, p // tm),
            in_specs=[pl.BlockSpec((tm, d), lambda j, i, te, tv: (i, 0)), w_spec, w_spec],
            out_specs=pl.BlockSpec((tm, tf), lambda j, i, te, tv: (i, j)),
            scratch_shapes=[pltpu.VMEM((d, tf), BF16), pltpu.VMEM((d, tf), BF16)]),
        out_shape=jax.ShapeDtypeStruct((p, f), BF16),
        compiler_params=_params("arbitrary", "arbitrary"),
        name="moe_expert_up",
    )(tile_expert, tile_valid, xs, w_gate, w_up)


def _expert_down_kernel(te_ref, tv_ref, a_ref, wd_ref, o_ref, wd16_ref):
    i = pl.program_id(1)

    @pl.when(_expert_changed(te_ref, i))
    def _():
        wd16_ref[...] = wd_ref[...].astype(BF16)

    @pl.when(tv_ref[i] > 0)
    def _():
        o_ref[...] = jnp.dot(a_ref[...], wd16_ref[...], preferred_element_type=F32)

    @pl.when(tv_ref[i] == 0)
    def _():
        o_ref[...] = jnp.zeros_like(o_ref)


def expert_down(act, w_down, layer, tile_expert, tile_valid, *, tm, tn=2048):
    p, f = act.shape
    d = w_down.shape[3]
    tn = min(tn, d)
    return pl.pallas_call(
        _expert_down_kernel,
        grid_spec=pltpu.PrefetchScalarGridSpec(
            num_scalar_prefetch=2,
            grid=(d // tn, p // tm),
            in_specs=[pl.BlockSpec((tm, f), lambda j, i, te, tv: (i, 0)),
                      pl.BlockSpec((None, None, f, tn), lambda j, i, te, tv: (layer, te[i], 0, j))],
            out_specs=pl.BlockSpec((tm, tn), lambda j, i, te, tv: (i, j)),
            scratch_shapes=[pltpu.VMEM((f, tn), BF16)]),
        out_shape=jax.ShapeDtypeStruct((p, d), F32),
        compiler_params=_params("arbitrary", "arbitrary"),
        name="moe_expert_down",
    )(tile_expert, tile_valid, act, w_down)


def _combine_kernel(pos_ref, x_ref, g_ref, w_ref, y_ref, o_ref, buf_ref, sem, *, tm):
    base = pl.program_id(0) * tm

    def row_copy(r, k):
        src = pos_ref[2 * (base + r) + k]
        return pltpu.make_async_copy(y_ref.at[pl.ds(src, 1), :], buf_ref.at[k, pl.ds(r, 1), :], sem)

    def start(r, carry):
        row_copy(r, 0).start()
        row_copy(r, 1).start()
        return carry

    def wait(r, carry):
        row_copy(r, 0).wait()
        row_copy(r, 1).wait()
        return carry

    lax.fori_loop(0, tm, start, 0)
    lax.fori_loop(0, tm, wait, 0)
    w = w_ref[...]
    y = w[:, 0:1] * buf_ref[0] + w[:, 1:2] * buf_ref[1]
    o_ref[...] = x_ref[...] + g_ref[...] * y


def moe_combine(x, g, wts, y, pos_flat, *, tm=256):
    t, d = x.shape
    tm = min(tm, t)
    return pl.pallas_call(
        functools.partial(_combine_kernel, tm=tm),
        grid_spec=pltpu.PrefetchScalarGridSpec(
            num_scalar_prefetch=1,
            grid=(t // tm,),
            in_specs=[pl.BlockSpec((tm, d), lambda i, pos: (i, 0)),
                      pl.BlockSpec((1, d), lambda i, pos: (0, 0)),
                      pl.BlockSpec((tm, 2), lambda i, pos: (i, 0)),
                      pl.BlockSpec(memory_space=pl.ANY)],
            out_specs=pl.BlockSpec((tm, d), lambda i, pos: (i, 0)),
            scratch_shapes=[pltpu.VMEM((2, tm, d), F32), pltpu.SemaphoreType.DMA(())]),
        out_shape=jax.ShapeDtypeStruct((t, d), F32),
        compiler_params=_params("arbitrary"),
        name="moe_combine",
    )(pos_flat, x, g, wts, y)


MOE_ROW_TILE = 256


def moe_ffn_residual(x, a, b, g, router_w, router_b, w_gate, w_up, w_down, layer):
    t, d = x.shape
    ne, tm = N_EXPERTS, MOE_ROW_TILE
    h, info, cnt = moe_router(x, a, b, router_w, router_b)
    counts = cnt[:, 0].astype(I32)
    padded = (counts + tm - 1) // tm * tm
    ends = jnp.cumsum(padded)
    starts = ends - padded
    n_tiles = 2 * t // tm + ne
    tile_row = jnp.arange(n_tiles, dtype=I32) * tm
    tile_expert = jnp.minimum(jnp.sum((tile_row[:, None] >= ends[None, :]).astype(I32), axis=1), ne - 1)
    tile_valid = (tile_row < ends[-1]).astype(I32)
    experts = info[0:2].astype(I32)
    start_of = jnp.sum(jnp.where(experts[..., None] == jnp.arange(ne, dtype=I32), starts, 0), axis=-1)
    pos = (start_of + info[4:6].astype(I32)).T.reshape(-1)
    wts = info[2:4].T
    xs = scatter_rows(h, pos, n_tiles * tm)
    act = expert_up(xs, w_gate, w_up, layer, tile_expert, tile_valid, tm=tm)
    y = expert_down(act, w_down, layer, tile_expert, tile_valid, tm=tm)
    return moe_combine(x, g, wts, y, pos)


MOBA_W = MOBA_HEADS * HEAD_DIM
MLSTM_QK_W = MLSTM_HEADS * MLSTM_QK_DIM
MLSTM_V_W = MLSTM_HEADS * MLSTM_V_DIM
EVEN_MAIN_W = 3 * MOBA_W + 2 * MLSTM_QK_W + 2 * MLSTM_V_W
NSA_Q_W = NSA_HEADS * HEAD_DIM
NSA_KV_W = NSA_KV_HEADS * HEAD_DIM
NSA_GATE_W = 3 * NSA_HEADS
DIL_GROUP_W = DIL_HEADS * HEAD_DIM
DIL_W = len(DIL_PATTERNS) * DIL_GROUP_W
ODD_GATE_COL0 = NSA_Q_W + 6 * NSA_KV_W


def _pad_cols(w, width):
    return jnp.pad(w, ((0, 0), (0, width - w.shape[1])))


def moba_mlstm_mixer(x, a, b, g, w_in, w_out, qn_g, kn_g, i_b, f_b, out_g):
    w_main = w_in[:, :EVEN_MAIN_W].astype(BF16)
    w_gate = _pad_cols(w_in[:, EVEN_MAIN_W:], LANES).astype(BF16)
    z, zg = norm_matmul(x, a, b, w_main, w_gate)
    kn, kmean = colprep(z, MOBA_W, MOBA_HEADS, kn_g, mean_rows=MOBA_BLOCK)
    vt = colprep(z, 2 * MOBA_W, MOBA_HEADS, transpose=True)
    o_a = moba_attention(z, 0, qn_g, kn, vt, kmean)
    c_q = 3 * MOBA_W
    c_k = c_q + MLSTM_QK_W
    c_v = c_k + MLSTM_QK_W
    c_o = c_v + MLSTM_V_W
    gate_bias = _pad_cols(jnp.concatenate([i_b, f_b]).reshape(1, -1), LANES)
    o_b = mlstm(z, c_q, c_k, c_v, c_o, zg, gate_bias, out_g)
    o = jnp.concatenate([o_a, o_b], axis=1)
    return matmul_residual(o, w_out.astype(BF16), x, g)


def nsa_dilated_mixer(x, a, b, g, w_in, w_out, nsa_qn_g, nsa_kn_g, pe_k, phik_w1, phik_b1, phik_w2,
                      pe_v, phiv_w1, phiv_b1, phiv_w2, dil_qn_g, dil_kn_g):
    t = x.shape[0]
    hd, kvw = HEAD_DIM, NSA_KV_W
    gc = ODD_GATE_COL0
    w_main = jnp.concatenate([w_in[:, :gc], w_in[:, gc + NSA_GATE_W:]], axis=1).astype(BF16)
    w_gate = _pad_cols(w_in[:, gc:gc + NSA_GATE_W], LANES).astype(BF16)
    z, zg = norm_matmul(x, a, b, w_main, w_gate)
    r_heads = NSA_HEADS // NSA_KV_HEADS
    gates_t = zg[:, :NSA_GATE_W].reshape(t, NSA_KV_HEADS, r_heads, 3).transpose(1, 3, 2, 0)
    c_kc = NSA_Q_W

    def cmp_blocks(col0):
        s = NSA_CMP_STRIDE
        return (z[:, col0:col0 + kvw].reshape(t // s, s, NSA_KV_HEADS, hd)
                .transpose(2, 0, 1, 3).reshape(NSA_KV_HEADS, t // s, s * hd))

    kcmp = nsa_compress(cmp_blocks(c_kc), pe_k, phik_w1, phik_b1, phik_w2, nsa_kn_g, is_key=True)
    vcmp = nsa_compress(cmp_blocks(c_kc + kvw), pe_v, phiv_w1, phiv_b1, phiv_w2, nsa_kn_g, is_key=False)
    ksn = colprep(z, c_kc + 2 * kvw, NSA_KV_HEADS, nsa_kn_g)
    vs_t = colprep(z, c_kc + 3 * kvw, NSA_KV_HEADS, transpose=True)
    kwn = colprep(z, c_kc + 4 * kvw, NSA_KV_HEADS, nsa_kn_g)
    vw_t = colprep(z, c_kc + 5 * kvw, NSA_KV_HEADS, transpose=True)
    o_nsa = nsa_attention(z, 0, nsa_qn_g, gates_t, kcmp, vcmp, ksn, vs_t, kwn, vw_t)
    c_dq = gc
    n_dil = len(DIL_PATTERNS) * DIL_HEADS
    dqn = colprep(z, c_dq, n_dil, dil_qn_g, scale=hd ** -0.5)
    dkn = colprep(z, c_dq + DIL_W, n_dil, dil_kn_g)
    dvb = colprep(z, c_dq + 2 * DIL_W, n_dil)
    outs, lses = [], []
    for gi, (win, dil) in enumerate(DIL_PATTERNS):
        grp = lambda arr: arr[:, gi * DIL_GROUP_W:(gi + 1) * DIL_GROUP_W].reshape(t // dil, dil * DIL_GROUP_W)
        o_g, lse_g = dilated_group(grp(dqn), grp(dkn), grp(dvb), dil, back=win // dil)
        outs.append(o_g.reshape(t, DIL_GROUP_W))
        lses.append(lse_g.reshape(t, DIL_GROUP_W))
    o_dil = dilated_merge(outs, lses)
    o = jnp.concatenate([o_nsa, o_dil], axis=1)
    return matmul_residual(o, w_out.astype(BF16), x, g)


def kernel(x, c, ada_w, ada_b, norm_mix_g, norm_ffn_g, ev_w_in, ev_w_out, moba_qn_g, moba_kn_g, mlstm_i_b, mlstm_f_b, mlstm_out_g, od_w_in, od_w_out, nsa_qn_g, nsa_kn_g, nsa_pe_k, nsa_phik_w1, nsa_phik_b1, nsa_phik_w2, nsa_pe_v, nsa_phiv_w1, nsa_phiv_b1, nsa_phiv_w2, dil_qn_g, dil_kn_g, router_w, router_b, moe_w_gate, moe_w_up, moe_w_down):
    bsz, t, d = x.shape
    assert bsz == 1, "kernels are written for a single sequence"
    depth = ada_w.shape[0]
    mod = adaln(c, ada_w, ada_b)
    xs = x.reshape(t, d)
    for layer in range(depth):
        sh_m, sc_m, g_m, sh_f, sc_f, g_f = [m.reshape(1, d) for m in jnp.split(mod[layer], 6)]
        a_m = norm_mix_g[layer].reshape(1, d) * (1.0 + sc_m)
        a_f = norm_ffn_g[layer].reshape(1, d) * (1.0 + sc_f)
        j = layer // 2
        if layer % 2 == 0:
            xs = moba_mlstm_mixer(xs, a_m, sh_m, g_m, ev_w_in[j], ev_w_out[j], moba_qn_g[j], moba_kn_g[j],
                                  mlstm_i_b[j], mlstm_f_b[j], mlstm_out_g[j])
        else:
            xs = nsa_dilated_mixer(xs, a_m, sh_m, g_m, od_w_in[j], od_w_out[j], nsa_qn_g[j], nsa_kn_g[j],
                                   nsa_pe_k[j], nsa_phik_w1[j], nsa_phik_b1[j], nsa_phik_w2[j],
                                   nsa_pe_v[j], nsa_phiv_w1[j], nsa_phiv_b1[j], nsa_phiv_w2[j],
                                   dil_qn_g[j], dil_kn_g[j])
        xs = moe_ffn_residual(xs, a_f, sh_f, g_f, router_w, router_b, moe_w_gate, moe_w_up, moe_w_down, layer)
    return xs.reshape(bsz, t, d)
```

```python
import functools

import numpy as np
import jax
import jax.numpy as jnp
from jax import lax
from jax.experimental import pallas as pl
from jax.experimental.pallas import tpu as pltpu

F32 = jnp.float32
BF16 = jnp.bfloat16
I32 = jnp.int32

NORM_EPS = 1e-6
HEAD_DIM = 128
LANES = 128
VMEM_LIMIT_BYTES = 56 * 1024 * 1024
NEG = -1e30
LOG2E = 1.4426950408889634

MOBA_HEADS = 16
MOBA_BLOCK = 256
MOBA_TOPK = 3
MLSTM_HEADS = 4
MLSTM_QK_DIM = 256
MLSTM_V_DIM = 512
MLSTM_GATE_CAP = 15.0
NSA_HEADS = 16
NSA_KV_HEADS = 4
NSA_CMP_BLOCK = 32
NSA_CMP_STRIDE = 16
NSA_SLC_BLOCK = 64
NSA_SLC_TOPN = 16
NSA_WINDOW = 512
DIL_PATTERNS = ((128, 1), (512, 4), (2048, 16))
DIL_HEADS = 8
N_EXPERTS = 16
N_EXPERT_GROUPS = 4
EXPERTS_PER_GROUP = N_EXPERTS // N_EXPERT_GROUPS
D_FF_EXPERT = 1024

_NT = (((1,), (1,)), ((), ()))
_TN = (((0,), (0,)), ((), ()))


def _params(*sem):
    return pltpu.CompilerParams(dimension_semantics=sem, vmem_limit_bytes=VMEM_LIMIT_BYTES)


def _split_bf16(a):
    hi = a.astype(BF16)
    lo = (a - hi.astype(F32)).astype(BF16)
    return hi, lo


def _dot_nt_f32(a, b):
    a_hi, a_lo = _split_bf16(a)
    b_hi, b_lo = _split_bf16(b)
    d = functools.partial(lax.dot_general, dimension_numbers=_NT, preferred_element_type=F32)
    return d(a_hi, b_hi) + (d(a_hi, b_lo) + d(a_lo, b_hi))


def _rms(x, gain):
    return x * lax.rsqrt(jnp.mean(x * x, axis=-1, keepdims=True) + NORM_EPS) * gain


def _adaln_kernel(c_ref, w_ref, b_ref, o_ref):
    c = c_ref[...]
    ca = c * jax.nn.sigmoid(c)
    o_ref[...] = jnp.sum(w_ref[...] * ca, axis=0, keepdims=True) + b_ref[...]


def adaln(c, ada_w, ada_b):
    depth, d, n = ada_w.shape
    tn = 512
    out = pl.pallas_call(
        _adaln_kernel,
        grid=(depth, n // tn),
        in_specs=[pl.BlockSpec((d, 1), lambda l, j: (0, 0)),
                  pl.BlockSpec((None, d, tn), lambda l, j: (l, 0, j)),
                  pl.BlockSpec((None, 1, tn), lambda l, j: (l, 0, j))],
        out_specs=pl.BlockSpec((None, 1, tn), lambda l, j: (l, 0, j)),
        out_shape=jax.ShapeDtypeStruct((depth, 1, n), F32),
        compiler_params=_params("parallel", "parallel"),
        name="adaln",
    )(c.reshape(d, 1), ada_w, ada_b.reshape(depth, 1, n))
    return out.reshape(depth, n)


NORM_ROW_CHUNK = 128


def _norm_matmul_kernel(x_ref, a_ref, b_ref, w_ref, ws_ref, o_ref, os_ref, h_ref):
    @pl.when(pl.program_id(1) == 0)
    def _():
        rows = min(NORM_ROW_CHUNK, x_ref.shape[0])

        def chunk(c, carry):
            sl = pl.ds(pl.multiple_of(c * rows, rows), rows)
            h_ref[sl, :] = (_rms(x_ref[sl, :], a_ref[...]) + b_ref[...]).astype(BF16)
            return carry

        lax.fori_loop(0, x_ref.shape[0] // rows, chunk, 0)
        os_ref[...] = jnp.dot(h_ref[...], ws_ref[...], preferred_element_type=F32)

    o_ref[...] = jnp.dot(h_ref[...], w_ref[...], preferred_element_type=F32).astype(o_ref.dtype)


def norm_matmul(x, a, b, w, w_side, *, tm=1024, tn=512, out_dtype=F32):
    t, d = x.shape
    n = w.shape[1]
    ns = w_side.shape[1]
    tm, tn = min(tm, t), min(tn, n)
    return pl.pallas_call(
        _norm_matmul_kernel,
        grid=(t // tm, n // tn),
        in_specs=[pl.BlockSpec((tm, d), lambda i, j: (i, 0), pipeline_mode=pl.Buffered(1)),
                  pl.BlockSpec((1, d), lambda i, j: (0, 0)),
                  pl.BlockSpec((1, d), lambda i, j: (0, 0)),
                  pl.BlockSpec((d, tn), lambda i, j: (0, j)),
                  pl.BlockSpec((d, ns), lambda i, j: (0, 0))],
        out_specs=[pl.BlockSpec((tm, tn), lambda i, j: (i, j)),
                   pl.BlockSpec((tm, ns), lambda i, j: (i, 0))],
        out_shape=[jax.ShapeDtypeStruct((t, n), out_dtype), jax.ShapeDtypeStruct((t, ns), F32)],
        scratch_shapes=[pltpu.VMEM((tm, d), BF16)],
        compiler_params=_params("parallel", "arbitrary"),
        name="norm_matmul",
    )(x, a, b, w, w_side)


def _matmul_residual_kernel(a_ref, w_ref, x_ref, g_ref, o_ref):
    y = jnp.dot(a_ref[...], w_ref[...], preferred_element_type=F32)
    o_ref[...] = x_ref[...] + g_ref[...] * y


def matmul_residual(a, w, x, g, *, tm=1024, tn=512):
    t, k = a.shape
    d = w.shape[1]
    tm, tn = min(tm, t), min(tn, d)
    return pl.pallas_call(
        _matmul_residual_kernel,
        grid=(t // tm, d // tn),
        in_specs=[pl.BlockSpec((tm, k), lambda i, j: (i, 0)),
                  pl.BlockSpec((k, tn), lambda i, j: (0, j)),
                  pl.BlockSpec((tm, tn), lambda i, j: (i, j)),
                  pl.BlockSpec((1, tn), lambda i, j: (0, j))],
        out_specs=pl.BlockSpec((tm, tn), lambda i, j: (i, j)),
        out_shape=jax.ShapeDtypeStruct((t, d), F32),
        compiler_params=_params("parallel", "parallel"),
        name="matmul_residual",
    )(a, w, x, g)


COLPREP_HEADS = 4


def _colprep_kernel(z_ref, g_ref, *o_refs, norm, scale, transpose, mean_rows):
    hd = HEAD_DIM
    for h in range(z_ref.shape[1] // hd):
        cols = slice(h * hd, (h + 1) * hd)
        y = z_ref[:, cols]
        if norm:
            y = _rms(y, g_ref[...])
        if mean_rows:
            tm = y.shape[0]
            o_refs[1][:, cols] = jnp.mean(y.reshape(tm // mean_rows, mean_rows, hd), axis=1)
        if scale != 1.0:
            y = y * scale
        if transpose:
            o_refs[0][cols, :] = y.T.astype(o_refs[0].dtype)
        else:
            o_refs[0][:, cols] = y.astype(o_refs[0].dtype)


def colprep(z, col0, nheads, gain=None, *, scale=1.0, transpose=False, mean_rows=0, tm=2048,
            out_dtype=BF16):
    t = z.shape[0]
    tm = min(tm, t)
    hb = COLPREP_HEADS
    bw = hb * HEAD_DIM
    assert nheads % hb == 0 and col0 % bw == 0
    c0 = col0 // bw
    norm = gain is not None
    g = (gain if norm else jnp.ones((HEAD_DIM,), F32)).reshape(1, HEAD_DIM)
    w = nheads * HEAD_DIM
    if transpose:
        out_shape = [jax.ShapeDtypeStruct((w, t), out_dtype)]
        out_specs = [pl.BlockSpec((bw, tm), lambda i, h: (h, i))]
    else:
        out_shape = [jax.ShapeDtypeStruct((t, w), out_dtype)]
        out_specs = [pl.BlockSpec((tm, bw), lambda i, h: (i, h))]
    if mean_rows:
        out_shape.append(jax.ShapeDtypeStruct((t // mean_rows, w), F32))
        out_specs.append(pl.BlockSpec((tm // mean_rows, bw), lambda i, h: (i, h)))
    outs = pl.pallas_call(
        functools.partial(_colprep_kernel, norm=norm, scale=scale, transpose=transpose,
                          mean_rows=mean_rows),
        grid=(t // tm, nheads // hb),
        in_specs=[pl.BlockSpec((tm, bw), lambda i, h: (i, c0 + h)),
                  pl.BlockSpec((1, HEAD_DIM), lambda i, h: (0, 0))],
        out_specs=out_specs,
        out_shape=out_shape,
        compiler_params=_params("parallel", "parallel"),
        name="colprep",
    )(z, g)
    return outs if mean_rows else outs[0]


ACC_ROWS = HEAD_DIM + 16


def _with_ones_rows(vt):
    return jnp.concatenate([vt, jnp.ones((ACC_ROWS - HEAD_DIM, vt.shape[1]), vt.dtype)], axis=0)


def _flash_step(s_chunks, vt1, m, acc_ref, slot):
    m_new = m
    for s in s_chunks:
        m_new = jnp.maximum(m_new, jnp.max(s, axis=0, keepdims=True))
    alpha = jnp.exp2(m - m_new)
    ps = [jnp.exp2((s - m_new).astype(BF16)) for s in s_chunks]
    p16 = jnp.concatenate(ps, axis=0) if len(ps) > 1 else ps[0]
    acc_ref[slot] = alpha * acc_ref[slot] + jnp.dot(vt1, p16, preferred_element_type=F32)
    return m_new


def _moba_kernel(zq_ref, g_ref, kn_ref, vt_ref, km_ref, o_ref, bias_ref, acc_ref, *, tq, nb, topk):
    i = pl.program_id(1)
    qn = _rms(zq_ref[...], g_ref[...])
    qs = (qn * (HEAD_DIM ** -0.5 * LOG2E)).astype(BF16)
    pos = i * tq + lax.broadcasted_iota(I32, (1, tq), 1)
    cur = (pos // MOBA_BLOCK).astype(F32)
    bidx = lax.broadcasted_iota(I32, (nb, tq), 0).astype(F32)

    gate = jnp.where(bidx < cur, _dot_nt_f32(km_ref[...], qn), -jnp.inf)
    sel = jnp.zeros((nb, tq), F32)
    for _ in range(topk):
        mx = jnp.max(gate, axis=0, keepdims=True)
        is_max = (gate == mx) & (gate > -jnp.inf)
        first = jnp.min(jnp.where(is_max, bidx, float(nb)), axis=0, keepdims=True)
        pick = bidx == first
        sel = jnp.where(pick, 1.0, sel)
        gate = jnp.where(pick, -jnp.inf, gate)
    sel = jnp.where(bidx == cur, 1.0, sel)
    bias_ref[...] = jnp.where(sel > 0.5, 0.0, NEG)
    acc_ref[...] = jnp.zeros_like(acc_ref)
    nblk = tq // MOBA_BLOCK

    def scores(j, causal):
        off = pl.multiple_of(j * tq, tq)
        chunks = []
        for b in range(nblk):
            k = kn_ref[pl.ds(off + b * MOBA_BLOCK, MOBA_BLOCK), :]
            s = (lax.dot_general(k, qs, _NT, preferred_element_type=F32)
                 + bias_ref[pl.ds(j * nblk + b, 1), :])
            if causal:
                kpos = off + b * MOBA_BLOCK + lax.broadcasted_iota(I32, (MOBA_BLOCK, 1), 0)
                s = jnp.where(kpos <= pos, s, NEG)
            chunks.append(s)
        return chunks

    def update(j, chunks, m):
        off = pl.multiple_of(j * tq, tq)
        return _flash_step(chunks, _with_ones_rows(vt_ref[:, pl.ds(off, tq)]), m, acc_ref, 0)

    def pair(jj, m):
        j = odd + 2 * jj
        s0, s1 = scores(j, False), scores(j + 1, False)
        return update(j + 1, s1, update(j, s0, m))

    odd = i % 2
    m = jnp.full((1, tq), NEG, F32)
    m = lax.cond(odd == 1, lambda mm: update(0, scores(0, False), mm), lambda mm: mm, m)
    m = lax.fori_loop(0, i // 2, pair, m)
    update(i, scores(i, True), m)
    acc = acc_ref[0]
    o_ref[...] = (acc[:HEAD_DIM] / acc[HEAD_DIM:HEAD_DIM + 1]).T.astype(o_ref.dtype)


def moba_attention(z, q_col0, q_gain, kn, vt, kmean, *, tq=1024):
    t = z.shape[0]
    tq = min(tq, t)
    nb = t // MOBA_BLOCK
    h = MOBA_HEADS
    c0 = q_col0 // HEAD_DIM
    return pl.pallas_call(
        functools.partial(_moba_kernel, tq=tq, nb=nb, topk=min(MOBA_TOPK, nb)),
        grid=(h, t // tq),
        in_specs=[pl.BlockSpec((tq, HEAD_DIM), lambda hh, i: (i, c0 + hh)),
                  pl.BlockSpec((1, HEAD_DIM), lambda hh, i: (0, 0)),
                  pl.BlockSpec((t, HEAD_DIM), lambda hh, i: (0, hh)),
                  pl.BlockSpec((HEAD_DIM, t), lambda hh, i: (hh, 0)),
                  pl.BlockSpec((nb, HEAD_DIM), lambda hh, i: (0, hh))],
        out_specs=pl.BlockSpec((tq, HEAD_DIM), lambda hh, i: (i, hh)),
        out_shape=jax.ShapeDtypeStruct((t, h * HEAD_DIM), BF16),
        scratch_shapes=[pltpu.VMEM((nb, tq), F32), pltpu.VMEM((1, ACC_ROWS, tq), F32)],
        compiler_params=_params("parallel", "parallel"),
        name="moba",
    )(z, q_gain.reshape(1, HEAD_DIM), kn, vt, kmean)


def _mlstm_kernel(zq_ref, zk_ref, zv_ref, zo_ref, zg_ref, gb_ref, og_ref, o_ref, ct_ref, n_ref, m_ref,
                  *, chunk):
    dk, dv, cap = MLSTM_QK_DIM, MLSTM_V_DIM, MLSTM_GATE_CAP

    @pl.when(pl.program_id(0) == 0)
    def _():
        ct_ref[...] = jnp.zeros_like(ct_ref)
        n_ref[...] = jnp.zeros_like(n_ref)
        m_ref[...] = jnp.zeros_like(m_ref)

    pre = cap * jnp.tanh((zg_ref[...] + gb_ref[...]) / cap)
    log_f = jnp.minimum(pre, 0.0) - jnp.log1p(jnp.exp(-jnp.abs(pre)))
    row = lax.broadcasted_iota(I32, (chunk, chunk), 0)
    col = lax.broadcasted_iota(I32, (chunk, chunk), 1)
    causal = col <= row
    tri = causal.astype(BF16)
    p1 = log_f.astype(BF16)
    r1 = log_f - p1.astype(F32)
    p2 = r1.astype(BF16)
    p3 = (r1 - p2.astype(F32)).astype(BF16)
    mm = functools.partial(jnp.dot, preferred_element_type=F32)
    b = mm(tri, p1) + (mm(tri, p2) + mm(tri, p3))
    pre_t = pre.T
    b_t = b.T

    for h in range(MLSTM_HEADS):
        li_row = pre_t[h:h + 1, :]
        li_col = pre[:, h:h + 1]
        b_row = b_t[MLSTM_HEADS + h:MLSTM_HEADS + h + 1, :]
        b_col = b[:, MLSTM_HEADS + h:MLSTM_HEADS + h + 1]
        m_prev = m_ref[h][:, :1]
        log_d = jnp.where(causal, b_col - b_row + li_row, -jnp.inf)
        log_inter = b_col + m_prev
        m_s = jnp.maximum(log_inter, jnp.max(log_d, axis=1, keepdims=True))
        dmat = jnp.exp(log_d - m_s)
        inter_w = jnp.exp(log_inter - m_s)
        qf = zq_ref[:, h * dk:(h + 1) * dk] * dk ** -0.5
        kf = zk_ref[:, h * dk:(h + 1) * dk]
        q16 = qf.astype(BF16)
        v16 = zv_ref[:, h * dv:(h + 1) * dv].astype(BF16)
        s = lax.dot_general(q16, kf.astype(BF16), _NT, preferred_element_type=F32) * dmat
        ct = ct_ref[h]
        num = mm(s.astype(BF16), v16) + inter_w * mm(q16, ct.astype(BF16))
        den = (jnp.sum(s, axis=1, keepdims=True)
               + inter_w * jnp.sum(qf * n_ref[h], axis=1, keepdims=True))
        hh = num / jnp.maximum(jnp.abs(den), jnp.exp(-m_s))
        hn = _rms(hh, og_ref[:, h * dv:(h + 1) * dv])
        o_ref[:, h * dv:(h + 1) * dv] = (hn * jax.nn.sigmoid(zo_ref[:, h * dv:(h + 1) * dv])).astype(o_ref.dtype)
        b_end = b_col[chunk - 1:chunk, :]
        log_w = b_end - b_col + li_col
        m_new = jnp.maximum(b_end + m_prev, jnp.max(log_w, axis=0, keepdims=True))
        kw = kf * jnp.exp(log_w - m_new)
        decay = jnp.exp(b_end + m_prev - m_new)
        ct_ref[h] = decay * ct + lax.dot_general(kw.astype(BF16), v16, _TN, preferred_element_type=F32)
        n_ref[h] = decay * n_ref[h] + jnp.sum(kw, axis=0, keepdims=True)
        m_ref[h] = jnp.broadcast_to(m_new, (1, LANES))


def mlstm(z, q_col0, k_col0, v_col0, o_col0, zg, gate_bias, out_gain, *, chunk=256):
    t = z.shape[0]
    chunk = min(chunk, t)
    wk, wv = MLSTM_HEADS * MLSTM_QK_DIM, MLSTM_HEADS * MLSTM_V_DIM
    return pl.pallas_call(
        functools.partial(_mlstm_kernel, chunk=chunk),
        grid=(t // chunk,),
        in_specs=[pl.BlockSpec((chunk, wk), lambda c: (c, q_col0 // wk)),
                  pl.BlockSpec((chunk, wk), lambda c: (c, k_col0 // wk)),
                  pl.BlockSpec((chunk, wv), lambda c: (c, v_col0 // wv)),
                  pl.BlockSpec((chunk, wv), lambda c: (c, o_col0 // wv)),
                  pl.BlockSpec((chunk, LANES), lambda c: (c, 0)),
                  pl.BlockSpec((1, LANES), lambda c: (0, 0)),
                  pl.BlockSpec((1, wv), lambda c: (0, 0))],
        out_specs=pl.BlockSpec((chunk, wv), lambda c: (c, 0)),
        out_shape=jax.ShapeDtypeStruct((t, wv), BF16),
        scratch_shapes=[pltpu.VMEM((MLSTM_HEADS, MLSTM_QK_DIM, MLSTM_V_DIM), F32),
                        pltpu.VMEM((MLSTM_HEADS, 1, MLSTM_QK_DIM), F32),
                        pltpu.VMEM((MLSTM_HEADS, 1, LANES), F32)],
        compiler_params=_params("arbitrary"),
        name="mlstm",
    )(z, z, z, z, zg, gate_bias, out_gain.reshape(1, wv))


def _nsa_compress_kernel(a_ref, pe_ref, w1_ref, b1_ref, w2_ref, g_ref, o_ref, *, is_key):
    n, half = a_ref.shape
    w1 = w1_ref[...]
    bias = jnp.sum(w1 * pe_ref[...], axis=0, keepdims=True) + b1_ref[...]
    w1 = w1.astype(BF16)
    a16 = a_ref[...].astype(BF16)
    u = jnp.dot(a16, w1[:half], preferred_element_type=F32)
    v = jnp.dot(a16, w1[half:], preferred_element_type=F32)
    x = u + pltpu.roll(v, n - 1, 0) + bias
    hid = 0.5 * x * (1.0 + jnp.tanh(np.sqrt(2.0 / np.pi) * (x + 0.044715 * (x * x * x))))
    y = jnp.dot(hid.astype(BF16), w2_ref[...].astype(BF16), preferred_element_type=F32)
    if is_key:
        y = _rms(y, g_ref[...])
    o_ref[...] = y.astype(o_ref.dtype)


def nsa_compress(a, pe, w1, b1, w2, gain, *, is_key):
    g, n, half = a.shape
    hd = HEAD_DIM
    out_shape, out_spec = jax.ShapeDtypeStruct((g, n, hd), BF16), pl.BlockSpec((None, n, hd), lambda i: (i, 0, 0))
    full = lambda shape: pl.BlockSpec(shape, lambda i: (0,) * len(shape))
    return pl.pallas_call(
        functools.partial(_nsa_compress_kernel, is_key=is_key),
        grid=(g,),
        in_specs=[pl.BlockSpec((None, n, half), lambda i: (i, 0, 0)),
                  full((2 * half, 1)), full((2 * half, hd)), full((1, hd)), full((hd, hd)), full((1, hd))],
        out_specs=out_spec,
        out_shape=out_shape,
        compiler_params=_params("parallel"),
        name="nsa_compress",
    )(a, pe.reshape(2 * half, 1), w1, b1.reshape(1, hd), w2, gain.reshape(1, hd))


NSA_SLC_KEY_TILE = 1024
NSA_SLC_SUB_KEYS = 256


def _nsa_kernel(zq_ref, qg_ref, gate_ref, kc_ref, vc_ref, mt_ref, ks_ref, vst_ref, kw_ref, vwt_ref,
                o_ref, bias_ref, acc_ref, *, tq, nc, ns, ntop, kt):
    i = pl.program_id(1)
    r_heads = NSA_HEADS // NSA_KV_HEADS
    w = r_heads * tq
    hd = HEAD_DIM
    scale = hd ** -0.5 * LOG2E
    qs = jnp.concatenate(
        [(_rms(zq_ref[:, r * hd:(r + 1) * hd], qg_ref[...]) * scale).astype(BF16) for r in range(r_heads)],
        axis=0)
    pos_t = i * tq + lax.broadcasted_iota(I32, (1, tq), 1)
    pos = i * tq + (lax.broadcasted_iota(I32, (1, w), 1) & (tq - 1))

    s = lax.dot_general(kc_ref[...], qs, _NT, preferred_element_type=F32)
    cmp_end = lax.broadcasted_iota(I32, (nc, 1), 0) * NSA_CMP_STRIDE + (NSA_CMP_BLOCK - 1)
    cmask = cmp_end <= pos
    s = jnp.where(cmask, s, NEG)
    p = jnp.where(cmask, jnp.exp2(s - jnp.max(s, axis=0, keepdims=True)), 0.0)
    den = jnp.sum(p, axis=0, keepdims=True)
    p = p * (1.0 / jnp.where(den > 0.0, den, 1.0))
    o_cmp = lax.dot_general(vc_ref[...], p.astype(BF16), _TN, preferred_element_type=F32)
    imp = p[:, 0:tq]
    for r in range(1, r_heads):
        imp = imp + p[:, r * tq:(r + 1) * tq]
    imp_hi, imp_lo = _split_bf16(imp)
    p_slc = (jnp.dot(mt_ref[...], imp_hi, preferred_element_type=F32)
             + jnp.dot(mt_ref[...], imp_lo, preferred_element_type=F32))

    blk = lax.broadcasted_iota(I32, (ns, tq), 0).astype(F32)
    cur = (pos_t // NSA_SLC_BLOCK).astype(F32)
    forced = (blk == 0.0) | (blk == cur) | (blk == cur - 1.0)
    score0 = jnp.where(blk > cur, -jnp.inf, jnp.where(forced, jnp.inf, p_slc))

    blk_l = blk[:, :LANES]

    def pick_one(_, carry):
        score, sel = carry
        mx = jnp.max(score, axis=0, keepdims=True)
        is_max = (score == mx) & (score > -jnp.inf)
        first = jnp.min(jnp.where(is_max, blk_l, float(ns)), axis=0, keepdims=True)
        pick = blk_l == first
        return jnp.where(pick, -jnp.inf, score), jnp.where(pick, 1.0, sel)

    sel = jnp.concatenate(
        [lax.fori_loop(0, ntop, pick_one,
                       (score0[:, c * LANES:(c + 1) * LANES], jnp.zeros((ns, LANES), F32)))[1]
         for c in range(tq // LANES)], axis=1)
    bias = jnp.where(sel > 0.5, 0.0, NEG)
    bias_ref[...] = jnp.concatenate([bias] * r_heads, axis=1)
    acc_ref[...] = jnp.zeros_like(acc_ref)

    sub = NSA_SLC_SUB_KEYS
    per_sub = sub // NSA_SLC_BLOCK

    def slc_scores(j, causal):
        off = pl.multiple_of(j * kt, kt)
        blk0 = j * (kt // NSA_SLC_BLOCK)
        chunks = []
        for c in range(kt // sub):
            s = lax.dot_general(ks_ref[pl.ds(off + c * sub, sub), :], qs, _NT, preferred_element_type=F32)
            for b in range(per_sub):
                r = c * per_sub + b
                sb = s[b * NSA_SLC_BLOCK:(b + 1) * NSA_SLC_BLOCK, :] + bias_ref[pl.ds(blk0 + r, 1), :]
                if causal:
                    kpos = off + r * NSA_SLC_BLOCK + lax.broadcasted_iota(I32, (NSA_SLC_BLOCK, 1), 0)
                    sb = jnp.where(kpos <= pos, sb, NEG)
                chunks.append(sb)
        return chunks

    def slc_update(j, chunks, m):
        off = pl.multiple_of(j * kt, kt)
        return _flash_step(chunks, _with_ones_rows(vst_ref[:, pl.ds(off, kt)]), m, acc_ref, 0)

    def slc_pair(jj, m):
        j = odd + 2 * jj
        s0, s1 = slc_scores(j, False), slc_scores(j + 1, False)
        return slc_update(j + 1, s1, slc_update(j, s0, m))

    n_past = (i * tq) // kt
    odd = n_past % 2
    m = jnp.full((1, w), NEG, F32)
    m = lax.cond(odd == 1, lambda mm: slc_update(0, slc_scores(0, False), mm), lambda mm: mm, m)
    m = lax.fori_loop(0, n_past // 2, slc_pair, m)
    slc_update(n_past, slc_scores(n_past, True), m)

    wk = NSA_WINDOW + tq
    off = pl.multiple_of(jnp.maximum(i * tq - NSA_WINDOW, 0), tq)
    s = lax.dot_general(kw_ref[pl.ds(off, wk), :], qs, _NT, preferred_element_type=F32)
    kpos = off + lax.broadcasted_iota(I32, (wk, 1), 0)
    s = jnp.where((kpos <= pos) & (kpos > pos - NSA_WINDOW), s, NEG)
    p = jnp.exp2(s - jnp.max(s, axis=0, keepdims=True))
    o_win = (jnp.dot(vwt_ref[:, pl.ds(off, wk)], p.astype(BF16), preferred_element_type=F32)
             / jnp.sum(p, axis=0, keepdims=True))

    gates = jax.nn.sigmoid(gate_ref[...])
    acc = acc_ref[0]
    o_slc = acc[:hd] / acc[hd:hd + 1]
    for r in range(r_heads):
        cols = slice(r * tq, (r + 1) * tq)
        o = (gates[0, r:r + 1, :] * o_cmp[:, cols] + gates[1, r:r + 1, :] * o_slc[:, cols]
             + gates[2, r:r + 1, :] * o_win[:, cols])
        o_ref[:, r * hd:(r + 1) * hd] = o.T.astype(o_ref.dtype)


def nsa_slc_weights(ns, nc):
    ratio = NSA_SLC_BLOCK // NSA_CMP_STRIDE
    m = np.zeros((ns, nc), np.float32)
    for j in range(ns):
        for d, wgt in [(-1, 0.5)] + [(k, 1.0) for k in range(ratio - 1)] + [(ratio - 1, 0.5)]:
            n = ratio * j + d
            if 0 <= n < nc - 1:
                m[j, n] = wgt
    return m


def nsa_attention(z, q_col0, q_gain, gates_t, kcmp, vcmp, ksn, vs_t, kwn, vw_t, *, tq=256):
    t = z.shape[0]
    g, r_heads, hd = NSA_KV_HEADS, NSA_HEADS // NSA_KV_HEADS, HEAD_DIM
    nc, ns = t // NSA_CMP_STRIDE, t // NSA_SLC_BLOCK
    mt = jnp.asarray(nsa_slc_weights(ns, nc), BF16)
    qw = r_heads * hd
    once = pl.Buffered(1)
    return pl.pallas_call(
        functools.partial(_nsa_kernel, tq=tq, nc=nc, ns=ns, ntop=min(NSA_SLC_TOPN, ns),
                          kt=min(NSA_SLC_KEY_TILE, t)),
        grid=(g, t // tq),
        in_specs=[pl.BlockSpec((tq, qw), lambda gg, i: (i, q_col0 // qw + gg)),
                  pl.BlockSpec((1, hd), lambda gg, i: (0, 0)),
                  pl.BlockSpec((None, 3, r_heads, tq), lambda gg, i: (gg, 0, 0, i)),
                  pl.BlockSpec((None, nc, hd), lambda gg, i: (gg, 0, 0)),
                  pl.BlockSpec((None, nc, hd), lambda gg, i: (gg, 0, 0)),
                  pl.BlockSpec((ns, nc), lambda gg, i: (0, 0)),
                  pl.BlockSpec((t, hd), lambda gg, i: (0, gg), pipeline_mode=once),
                  pl.BlockSpec((hd, t), lambda gg, i: (gg, 0), pipeline_mode=once),
                  pl.BlockSpec((t, hd), lambda gg, i: (0, gg), pipeline_mode=once),
                  pl.BlockSpec((hd, t), lambda gg, i: (gg, 0), pipeline_mode=once)],
        out_specs=pl.BlockSpec((tq, qw), lambda gg, i: (i, gg)),
        out_shape=jax.ShapeDtypeStruct((t, NSA_HEADS * hd), BF16),
        scratch_shapes=[pltpu.VMEM((ns, r_heads * tq), F32), pltpu.VMEM((1, ACC_ROWS, r_heads * tq), F32)],
        compiler_params=_params("parallel", "arbitrary"),
        name="nsa",
    )(z, q_gain.reshape(1, hd), gates_t, kcmp, vcmp, mt, ksn, vs_t, kwn, vw_t)


def _dilated_kernel(q_ref, kp_ref, kc_ref, vp_ref, vc_ref, o_ref, lse_ref, *, tq, back):
    i = pl.program_id(1)
    hd = HEAD_DIM
    qpos = i * tq + lax.broadcasted_iota(I32, (tq, 1), 0)
    kpos = (i - 1) * tq + lax.broadcasted_iota(I32, (1, 2 * tq), 1)
    mask = (kpos <= qpos) & (kpos >= qpos - back) & (kpos >= 0)
    for j in range(DIL_HEADS):
        cols = slice(j * hd, (j + 1) * hd)
        k = jnp.concatenate([kp_ref[:, cols], kc_ref[:, cols]], axis=0)
        v = jnp.concatenate([vp_ref[:, cols], vc_ref[:, cols]], axis=0)
        s = lax.dot_general(q_ref[:, cols], k, _NT, preferred_element_type=F32)
        s = jnp.where(mask, s, NEG)
        m = jnp.max(s, axis=1, keepdims=True)
        p = jnp.exp(s - m)
        l = jnp.sum(p, axis=1, keepdims=True)
        o_ref[:, cols] = jnp.dot(p.astype(BF16), v, preferred_element_type=F32) / l
        lse_ref[:, cols] = jnp.broadcast_to(m + jnp.log(l), (tq, hd))


def dilated_group(qd, kd, vd, dil, *, back, tq=128):
    td, wd = qd.shape
    hw = wd // dil
    tq = min(tq, td)
    cur = pl.BlockSpec((tq, hw), lambda r, i: (i, r))
    prev = pl.BlockSpec((tq, hw), lambda r, i: (jnp.maximum(i - 1, 0), r))
    return pl.pallas_call(
        functools.partial(_dilated_kernel, tq=tq, back=back),
        grid=(dil, td // tq),
        in_specs=[cur, prev, cur, prev, cur],
        out_specs=[cur, cur],
        out_shape=[jax.ShapeDtypeStruct((td, wd), F32), jax.ShapeDtypeStruct((td, wd), F32)],
        compiler_params=_params("parallel", "parallel"),
        name="dilated",
    )(qd, kd, kd, vd, vd)


def _dilated_merge_kernel(*refs):
    n = (len(refs) - 1) // 2
    o_refs, lse_refs, out_ref = refs[:n], refs[n:2 * n], refs[2 * n]
    lse = [r[...] for r in lse_refs]
    m = functools.reduce(jnp.maximum, lse)
    e = [jnp.exp(x - m) for x in lse]
    tot = functools.reduce(lambda a, b: a + b, e)
    acc = sum(ei * r[...] for ei, r in zip(e, o_refs))
    out_ref[...] = (acc / tot).astype(out_ref.dtype)


def dilated_merge(outs, lses, *, tm=512):
    t, wd = outs[0].shape
    tm = min(tm, t)
    spec = pl.BlockSpec((tm, wd), lambda i: (i, 0))
    return pl.pallas_call(
        _dilated_merge_kernel,
        grid=(t // tm,),
        in_specs=[spec] * (2 * len(outs)),
        out_specs=spec,
        out_shape=jax.ShapeDtypeStruct((t, wd), BF16),
        compiler_params=_params("parallel"),
        name="dilated_merge",
    )(*outs, *lses)


def _top2_of_4(v):
    best, loc0 = v[0], jnp.zeros_like(v[0])
    for j in range(1, 4):
        better = v[j] > best
        best = jnp.where(better, v[j], best)
        loc0 = jnp.where(better, float(j), loc0)
    best1, loc1 = jnp.full_like(v[0], -jnp.inf), jnp.zeros_like(v[0])
    for j in range(4):
        better = (v[j] > best1) & (loc0 != float(j))
        best1 = jnp.where(better, v[j], best1)
        loc1 = jnp.where(better, float(j), loc1)
    return loc0, loc1


U32 = jnp.uint32


def _pack_bf16_halves(h):
    bits = lax.bitcast_convert_type(h.astype(BF16).astype(F32), U32)
    half = h.shape[1] // 2
    return bits[:, half:] | (bits[:, :half] >> 16)


def _unpack_bf16_halves(packed):
    lo = lax.bitcast_convert_type(packed << 16, F32).astype(BF16)
    hi = lax.bitcast_convert_type(packed & jnp.uint32(0xFFFF0000), F32).astype(BF16)
    return lo, hi


def _router_kernel(x_ref, a_ref, b_ref, wt_ref, rb_ref, h_ref, info_ref, cnt_ref, carry_ref, *, tm):
    ne, ng, eg = N_EXPERTS, N_EXPERT_GROUPS, EXPERTS_PER_GROUP

    @pl.when(pl.program_id(0) == 0)
    def _():
        carry_ref[...] = jnp.zeros_like(carry_ref)

    h = _rms(x_ref[...], a_ref[...]) + b_ref[...]
    h_ref[...] = _pack_bf16_halves(h)
    s = jax.nn.sigmoid(_dot_nt_f32(wt_ref[...], h))
    sel = s + rb_ref[...]
    srow = [s[e:e + 1, :] for e in range(ne)]
    selrow = [sel[e:e + 1, :] for e in range(ne)]
    gscore = []
    for g in range(ng):
        a0, a1, a2, a3 = selrow[eg * g:eg * g + 4]
        hi1, lo1, hi2, lo2 = jnp.maximum(a0, a1), jnp.minimum(a0, a1), jnp.maximum(a2, a3), jnp.minimum(a2, a3)
        gscore.append(jnp.maximum(hi1, hi2) + jnp.maximum(jnp.minimum(hi1, hi2), jnp.maximum(lo1, lo2)))
    best, grp = gscore[0], jnp.zeros_like(gscore[0])
    for g in range(1, ng):
        better = gscore[g] > best
        best = jnp.where(better, gscore[g], best)
        grp = jnp.where(better, float(g), grp)
    pick = lambda rows, j: sum(jnp.where(grp == float(g), rows[eg * g + j], 0.0) for g in range(ng))
    loc0, loc1 = _top2_of_4([pick(selrow, j) for j in range(eg)])
    s_in = [pick(srow, j) for j in range(eg)]
    s0 = sum(jnp.where(loc0 == float(j), s_in[j], 0.0) for j in range(eg))
    s1 = sum(jnp.where(loc1 == float(j), s_in[j], 0.0) for j in range(eg))
    e0 = grp * eg + loc0
    e1 = grp * eg + loc1
    tot = s0 + s1
    eidx = lax.broadcasted_iota(I32, (ne, tm), 0).astype(F32)
    oh0 = eidx == e0
    oh1 = eidx == e1
    onehot = (oh0 | oh1).astype(BF16)
    upper = (lax.broadcasted_iota(I32, (tm, tm), 0) < lax.broadcasted_iota(I32, (tm, tm), 1)).astype(BF16)
    before = jnp.dot(onehot, upper, preferred_element_type=F32) + carry_ref[:, :1]
    r0 = jnp.sum(jnp.where(oh0, before, 0.0), axis=0, keepdims=True)
    r1 = jnp.sum(jnp.where(oh1, before, 0.0), axis=0, keepdims=True)
    carry_ref[...] = carry_ref[...] + jnp.sum(onehot.astype(F32), axis=1, keepdims=True)
    cnt_ref[...] = carry_ref[...]
    rid = lax.broadcasted_iota(I32, (8, tm), 0)
    rows = (e0, e1, s0 / tot, s1 / tot, r0, r1)
    info = jnp.zeros((8, tm), F32)
    for k, r in enumerate(rows):
        info = jnp.where(rid == k, r, info)
    info_ref[...] = info


def moe_router(x, a, b, router_w, router_b, *, tm=512):
    t, d = x.shape
    tm = min(tm, t)
    ne = N_EXPERTS
    return pl.pallas_call(
        functools.partial(_router_kernel, tm=tm),
        grid=(t // tm,),
        in_specs=[pl.BlockSpec((tm, d), lambda i: (i, 0)),
                  pl.BlockSpec((1, d), lambda i: (0, 0)),
                  pl.BlockSpec((1, d), lambda i: (0, 0)),
                  pl.BlockSpec((ne, d), lambda i: (0, 0)),
                  pl.BlockSpec((ne, 1), lambda i: (0, 0))],
        out_specs=[pl.BlockSpec((tm, d // 2), lambda i: (i, 0)),
                   pl.BlockSpec((8, tm), lambda i: (0, i)),
                   pl.BlockSpec((ne, LANES), lambda i: (0, 0))],
        out_shape=[jax.ShapeDtypeStruct((t, d // 2), U32),
                   jax.ShapeDtypeStruct((8, t), F32),
                   jax.ShapeDtypeStruct((ne, LANES), F32)],
        scratch_shapes=[pltpu.VMEM((ne, LANES), F32)],
        compiler_params=_params("arbitrary"),
        name="moe_router",
    )(x, a, b, router_w.T, router_b.reshape(ne, 1))


def _scatter_rows_kernel(pos_ref, h_ref, init_ref, o_ref, sem, *, tm):
    del init_ref
    base = pl.program_id(0) * tm

    def row_copy(r, k):
        dst = pos_ref[2 * (base + r) + k]
        return pltpu.make_async_copy(h_ref.at[pl.ds(r, 1), :], o_ref.at[pl.ds(dst, 1), :], sem)

    def start(r, carry):
        row_copy(r, 0).start()
        row_copy(r, 1).start()
        return carry

    def wait(r, carry):
        row_copy(r, 0).wait()
        row_copy(r, 1).wait()
        return carry

    lax.fori_loop(0, tm, start, 0)
    lax.fori_loop(0, tm, wait, 0)


def scatter_rows(h, pos_flat, n_rows, *, tm=256):
    t, d = h.shape
    tm = min(tm, t)
    return pl.pallas_call(
        functools.partial(_scatter_rows_kernel, tm=tm),
        grid_spec=pltpu.PrefetchScalarGridSpec(
            num_scalar_prefetch=1,
            grid=(t // tm,),
            in_specs=[pl.BlockSpec((tm, d), lambda i, pos: (i, 0)),
                      pl.BlockSpec(memory_space=pl.ANY)],
            out_specs=pl.BlockSpec(memory_space=pl.ANY),
            scratch_shapes=[pltpu.SemaphoreType.DMA(())]),
        out_shape=jax.ShapeDtypeStruct((n_rows, d), h.dtype),
        input_output_aliases={2: 0},
        compiler_params=_params("arbitrary"),
        name="moe_scatter",
    )(pos_flat, h, jnp.zeros((n_rows, d), h.dtype))


def _expert_changed(te_ref, i):
    return (i == 0) | (te_ref[i] != te_ref[jnp.maximum(i - 1, 0)])


def _expert_up_kernel(te_ref, tv_ref, xs_ref, wg_ref, wu_ref, o_ref, wg16_ref, wu16_ref):
    i = pl.program_id(1)

    @pl.when(_expert_changed(te_ref, i))
    def _():
        wg16_ref[...] = wg_ref[...].astype(BF16)
        wu16_ref[...] = wu_ref[...].astype(BF16)

    @pl.when(tv_ref[i] > 0)
    def _():
        lo, hi = _unpack_bf16_halves(xs_ref[...])
        half = lo.shape[1]
        mm = functools.partial(jnp.dot, preferred_element_type=F32)
        g = mm(lo, wg16_ref[:half, :]) + mm(hi, wg16_ref[half:, :])
        u = mm(lo, wu16_ref[:half, :]) + mm(hi, wu16_ref[half:, :])
        o_ref[...] = (g * jax.nn.sigmoid(g) * u).astype(o_ref.dtype)

    @pl.when(tv_ref[i] == 0)
    def _():
        o_ref[...] = jnp.zeros_like(o_ref)


def expert_up(xs, w_gate, w_up, layer, tile_expert, tile_valid, *, tm, tf=512):
    p, dp = xs.shape
    d = w_gate.shape[2]
    f = w_gate.shape[3]
    tf = min(tf, f)
    w_spec = pl.BlockSpec((None, None, d, tf), lambda j, i, te, tv: (layer, te[i], 0, j),
                          pipeline_mode=pl.Buffered(1))
    return pl.pallas_call(
        _expert_up_kernel,
        grid_spec=pltpu.PrefetchScalarGridSpec(
            num_scalar_prefetch=2,
            grid=(f // tf, p // tm),
            in_specs=[pl.BlockSpec((tm, dp), lambda j, i, te, tv: (i, 0)), w_spec, w_spec],
            out_specs=pl.BlockSpec((tm, tf), lambda j, i, te, tv: (i, j)),
            scratch_shapes=[pltpu.VMEM((d, tf), BF16), pltpu.VMEM((d, tf), BF16)]),
        out_shape=jax.ShapeDtypeStruct((p, f), BF16),
        compiler_params=_params("arbitrary", "arbitrary"),
        name="moe_expert_up",
    )(tile_expert, tile_valid, xs, w_gate, w_up)


def _expert_down_kernel(te_ref, tv_ref, a_ref, wd_ref, o_ref, wd16_ref):
    i = pl.program_id(1)

    @pl.when(_expert_changed(te_ref, i))
    def _():
        wd16_ref[...] = wd_ref[...].astype(BF16)

    @pl.when(tv_ref[i] > 0)
    def _():
        o_ref[...] = jnp.dot(a_ref[...], wd16_ref[...], preferred_element_type=F32)

    @pl.when(tv_ref[i] == 0)
    def _():
        o_ref[...] = jnp.zeros_like(o_ref)


def expert_down(act, w_down, layer, tile_expert, tile_valid, *, tm, tn=2048):
    p, f = act.shape
    d = w_down.shape[3]
    tn = min(tn, d)
    return pl.pallas_call(
        _expert_down_kernel,
        grid_spec=pltpu.PrefetchScalarGridSpec(
            num_scalar_prefetch=2,
            grid=(d // tn, p // tm),
            in_specs=[pl.BlockSpec((tm, f), lambda j, i, te, tv: (i, 0)),
                      pl.BlockSpec((None, None, f, tn), lambda j, i, te, tv: (layer, te[i], 0, j))],
            out_specs=pl.BlockSpec((tm, tn), lambda j, i, te, tv: (i, j)),
            scratch_shapes=[pltpu.VMEM((f, tn), BF16)]),
        out_shape=jax.ShapeDtypeStruct((p, d), F32),
        compiler_params=_params("arbitrary", "arbitrary"),
        name="moe_expert_down",
    )(tile_expert, tile_valid, act, w_down)


def _combine_kernel(pos_ref, x_ref, g_ref, w_ref, y_ref, o_ref, buf_ref, sem, *, tm):
    base = pl.program_id(0) * tm

    def row_copy(r, k):
        src = pos_ref[2 * (base + r) + k]
        return pltpu.make_async_copy(y_ref.at[pl.ds(src, 1), :], buf_ref.at[k, pl.ds(r, 1), :], sem)

    def start(r, carry):
        row_copy(r, 0).start()
        row_copy(r, 1).start()
        return carry

    def wait(r, carry):
        row_copy(r, 0).wait()
        row_copy(r, 1).wait()
        return carry

    lax.fori_loop(0, tm, start, 0)
    lax.fori_loop(0, tm, wait, 0)
    w = w_ref[...]
    y = w[:, 0:1] * buf_ref[0] + w[:, 1:2] * buf_ref[1]
    o_ref[...] = x_ref[...] + g_ref[...] * y


def moe_combine(x, g, wts, y, pos_flat, *, tm=256):
    t, d = x.shape
    tm = min(tm, t)
    return pl.pallas_call(
        functools.partial(_combine_kernel, tm=tm),
        grid_spec=pltpu.PrefetchScalarGridSpec(
            num_scalar_prefetch=1,
            grid=(t // tm,),
            in_specs=[pl.BlockSpec((tm, d), lambda i, pos: (i, 0)),
                      pl.BlockSpec((1, d), lambda i, pos: (0, 0)),
                      pl.BlockSpec((tm, 2), lambda i, pos: (i, 0)),
                      pl.BlockSpec(memory_space=pl.ANY)],
            out_specs=pl.BlockSpec((tm, d), lambda i, pos: (i, 0)),
            scratch_shapes=[pltpu.VMEM((2, tm, d), F32), pltpu.SemaphoreType.DMA(())]),
        out_shape=jax.ShapeDtypeStruct((t, d), F32),
        compiler_params=_params("arbitrary"),
        name="moe_combine",
    )(pos_flat, x, g, wts, y)


MOE_ROW_TILE = 256


def moe_ffn_residual(x, a, b, g, router_w, router_b, w_gate, w_up, w_down, layer):
    t, d = x.shape
    ne, tm = N_EXPERTS, MOE_ROW_TILE
    h, info, cnt = moe_router(x, a, b, router_w, router_b)
    counts = cnt[:, 0].astype(I32)
    padded = (counts + tm - 1) // tm * tm
    ends = jnp.cumsum(padded)
    starts = ends - padded
    n_tiles = 2 * t // tm + ne
    tile_row = jnp.arange(n_tiles, dtype=I32) * tm
    tile_expert = jnp.minimum(jnp.sum((tile_row[:, None] >= ends[None, :]).astype(I32), axis=1), ne - 1)
    tile_valid = (tile_row < ends[-1]).astype(I32)
    experts = info[0:2].astype(I32)
    start_of = jnp.sum(jnp.where(experts[..., None] == jnp.arange(ne, dtype=I32), starts, 0), axis=-1)
    pos = (start_of + info[4:6].astype(I32)).T.reshape(-1)
    wts = info[2:4].T
    xs = scatter_rows(h, pos, n_tiles * tm)
    act = expert_up(xs, w_gate, w_up, layer, tile_expert, tile_valid, tm=tm)
    y = expert_down(act, w_down, layer, tile_expert, tile_valid, tm=tm)
    return moe_combine(x, g, wts, y, pos)


MOBA_W = MOBA_HEADS * HEAD_DIM
MLSTM_QK_W = MLSTM_HEADS * MLSTM_QK_DIM
MLSTM_V_W = MLSTM_HEADS * MLSTM_V_DIM
EVEN_MAIN_W = 3 * MOBA_W + 2 * MLSTM_QK_W + 2 * MLSTM_V_W
NSA_Q_W = NSA_HEADS * HEAD_DIM
NSA_KV_W = NSA_KV_HEADS * HEAD_DIM
NSA_GATE_W = 3 * NSA_HEADS
DIL_GROUP_W = DIL_HEADS * HEAD_DIM
DIL_W = len(DIL_PATTERNS) * DIL_GROUP_W
ODD_GATE_COL0 = NSA_Q_W + 6 * NSA_KV_W


def _pad_cols(w, width):
    return jnp.pad(w, ((0, 0), (0, width - w.shape[1])))


def moba_mlstm_mixer(x, a, b, g, w_in, w_out, qn_g, kn_g, i_b, f_b, out_g):
    w_main = w_in[:, :EVEN_MAIN_W].astype(BF16)
    w_gate = _pad_cols(w_in[:, EVEN_MAIN_W:], LANES).astype(BF16)
    z, zg = norm_matmul(x, a, b, w_main, w_gate)
    kn, kmean = colprep(z, MOBA_W, MOBA_HEADS, kn_g, mean_rows=MOBA_BLOCK)
    vt = colprep(z, 2 * MOBA_W, MOBA_HEADS, transpose=True)
    o_a = moba_attention(z, 0, qn_g, kn, vt, kmean)
    c_q = 3 * MOBA_W
    c_k = c_q + MLSTM_QK_W
    c_v = c_k + MLSTM_QK_W
    c_o = c_v + MLSTM_V_W
    gate_bias = _pad_cols(jnp.concatenate([i_b, f_b]).reshape(1, -1), LANES)
    o_b = mlstm(z, c_q, c_k, c_v, c_o, zg, gate_bias, out_g)
    o = jnp.concatenate([o_a, o_b], axis=1)
    return matmul_residual(o, w_out.astype(BF16), x, g)


def nsa_dilated_mixer(x, a, b, g, w_in, w_out, nsa_qn_g, nsa_kn_g, pe_k, phik_w1, phik_b1, phik_w2,
                      pe_v, phiv_w1, phiv_b1, phiv_w2, dil_qn_g, dil_kn_g):
    t = x.shape[0]
    hd, kvw = HEAD_DIM, NSA_KV_W
    gc = ODD_GATE_COL0
    w_main = jnp.concatenate([w_in[:, :gc], w_in[:, gc + NSA_GATE_W:]], axis=1).astype(BF16)
    w_gate = _pad_cols(w_in[:, gc:gc + NSA_GATE_W], LANES).astype(BF16)
    z, zg = norm_matmul(x, a, b, w_main, w_gate)
    r_heads = NSA_HEADS // NSA_KV_HEADS
    gates_t = zg[:, :NSA_GATE_W].reshape(t, NSA_KV_HEADS, r_heads, 3).transpose(1, 3, 2, 0)
    c_kc = NSA_Q_W

    def cmp_blocks(col0):
        s = NSA_CMP_STRIDE
        return (z[:, col0:col0 + kvw].reshape(t // s, s, NSA_KV_HEADS, hd)
                .transpose(2, 0, 1, 3).reshape(NSA_KV_HEADS, t // s, s * hd))

    kcmp = nsa_compress(cmp_blocks(c_kc), pe_k, phik_w1, phik_b1, phik_w2, nsa_kn_g, is_key=True)
    vcmp = nsa_compress(cmp_blocks(c_kc + kvw), pe_v, phiv_w1, phiv_b1, phiv_w2, nsa_kn_g, is_key=False)
    ksn = colprep(z, c_kc + 2 * kvw, NSA_KV_HEADS, nsa_kn_g)
    vs_t = colprep(z, c_kc + 3 * kvw, NSA_KV_HEADS, transpose=True)
    kwn = colprep(z, c_kc + 4 * kvw, NSA_KV_HEADS, nsa_kn_g)
    vw_t = colprep(z, c_kc + 5 * kvw, NSA_KV_HEADS, transpose=True)
    o_nsa = nsa_attention(z, 0, nsa_qn_g, gates_t, kcmp, vcmp, ksn, vs_t, kwn, vw_t)
    c_dq = gc
    n_dil = len(DIL_PATTERNS) * DIL_HEADS
    dqn = colprep(z, c_dq, n_dil, dil_qn_g, scale=hd ** -0.5)
    dkn = colprep(z, c_dq + DIL_W, n_dil, dil_kn_g)
    dvb = colprep(z, c_dq + 2 * DIL_W, n_dil)
    outs, lses = [], []
    for gi, (win, dil) in enumerate(DIL_PATTERNS):
        grp = lambda arr: arr[:, gi * DIL_GROUP_W:(gi + 1) * DIL_GROUP_W].reshape(t // dil, dil * DIL_GROUP_W)
        o_g, lse_g = dilated_group(grp(dqn), grp(dkn), grp(dvb), dil, back=win // dil)
        outs.append(o_g.reshape(t, DIL_GROUP_W))
        lses.append(lse_g.reshape(t, DIL_GROUP_W))
    o_dil = dilated_merge(outs, lses)
    o = jnp.concatenate([o_nsa, o_dil], axis=1)
    return matmul_residual(o, w_out.astype(BF16), x, g)


def kernel(x, c, ada_w, ada_b, norm_mix_g, norm_ffn_g, ev_w_in, ev_w_out, moba_qn_g, moba_kn_g, mlstm_i_b, mlstm_f_b, mlstm_out_g, od_w_in, od_w_out, nsa_qn_g, nsa_kn_g, nsa_pe_k, nsa_phik_w1, nsa_phik_b1, nsa_phik_w2, nsa_pe_v, nsa_phiv_w1, nsa_phiv_b1, nsa_phiv_w2, dil_qn_g, dil_kn_g, router_w, router_b, moe_w_gate, moe_w_up, moe_w_down):
    bsz, t, d = x.shape
    assert bsz == 1, "kernels are written for a single sequence"
    depth = ada_w.shape[0]
    mod = adaln(c, ada_w, ada_b)
    xs = x.reshape(t, d)
    for layer in range(depth):
        sh_m, sc_m, g_m, sh_f, sc_f, g_f = [m.reshape(1, d) for m in jnp.split(mod[layer], 6)]
        a_m = norm_mix_g[layer].reshape(1, d) * (1.0 + sc_m)
        a_f = norm_ffn_g[layer].reshape(1, d) * (1.0 + sc_f)
        j = layer // 2
        if layer % 2 == 0:
            xs = moba_mlstm_mixer(xs, a_m, sh_m, g_m, ev_w_in[j], ev_w_out[j], moba_qn_g[j], moba_kn_g[j],
                                  mlstm_i_b[j], mlstm_f_b[j], mlstm_out_g[j])
        else:
            xs = nsa_dilated_mixer(xs, a_m, sh_m, g_m, od_w_in[j], od_w_out[j], nsa_qn_g[j], nsa_kn_g[j],
                                   nsa_pe_k[j], nsa_phik_w1[j], nsa_phik_b1[j], nsa_phik_w2[j],
                                   nsa_pe_v[j], nsa_phiv_w1[j], nsa_phiv_b1[j], nsa_phiv_w2[j],
                                   dil_qn_g[j], dil_kn_g[j])
        xs = moe_ffn_residual(xs, a_f, sh_f, g_f, router_w, router_b, moe_w_gate, moe_w_up, moe_w_down, layer)
    return xs.reshape(bsz, t, d)
```

```python
import functools

import numpy as np
import jax
import jax.numpy as jnp
from jax import lax
from jax.experimental import pallas as pl
from jax.experimental.pallas import tpu as pltpu

F32 = jnp.float32
BF16 = jnp.bfloat16
I32 = jnp.int32

NORM_EPS = 1e-6
HEAD_DIM = 128
LANES = 128
VMEM_LIMIT_BYTES = 56 * 1024 * 1024
NEG = -1e30
LOG2E = 1.4426950408889634

MOBA_HEADS = 16
MOBA_BLOCK = 256
MOBA_TOPK = 3
MLSTM_HEADS = 4
MLSTM_QK_DIM = 256
MLSTM_V_DIM = 512
MLSTM_GATE_CAP = 15.0
NSA_HEADS = 16
NSA_KV_HEADS = 4
NSA_CMP_BLOCK = 32
NSA_CMP_STRIDE = 16
NSA_SLC_BLOCK = 64
NSA_SLC_TOPN = 16
NSA_WINDOW = 512
DIL_PATTERNS = ((128, 1), (512, 4), (2048, 16))
DIL_HEADS = 8
N_EXPERTS = 16
N_EXPERT_GROUPS = 4
EXPERTS_PER_GROUP = N_EXPERTS // N_EXPERT_GROUPS
D_FF_EXPERT = 1024

_NT = (((1,), (1,)), ((), ()))
_TN = (((0,), (0,)), ((), ()))


def _params(*sem):
    return pltpu.CompilerParams(dimension_semantics=sem, vmem_limit_bytes=VMEM_LIMIT_BYTES)


def _split_bf16(a):
    hi = a.astype(BF16)
    lo = (a - hi.astype(F32)).astype(BF16)
    return hi, lo


def _dot_nt_f32(a, b):
    a_hi, a_lo = _split_bf16(a)
    b_hi, b_lo = _split_bf16(b)
    d = functools.partial(lax.dot_general, dimension_numbers=_NT, preferred_element_type=F32)
    return d(a_hi, b_hi) + (d(a_hi, b_lo) + d(a_lo, b_hi))


def _rms(x, gain):
    return x * lax.rsqrt(jnp.mean(x * x, axis=-1, keepdims=True) + NORM_EPS) * gain


def _adaln_kernel(c_ref, w_ref, b_ref, o_ref):
    c = c_ref[...]
    ca = c * jax.nn.sigmoid(c)
    o_ref[...] = jnp.sum(w_ref[...] * ca, axis=0, keepdims=True) + b_ref[...]


def adaln(c, ada_w, ada_b):
    depth, d, n = ada_w.shape
    tn = 512
    out = pl.pallas_call(
        _adaln_kernel,
        grid=(depth, n // tn),
        in_specs=[pl.BlockSpec((d, 1), lambda l, j: (0, 0)),
                  pl.BlockSpec((None, d, tn), lambda l, j: (l, 0, j)),
                  pl.BlockSpec((None, 1, tn), lambda l, j: (l, 0, j))],
        out_specs=pl.BlockSpec((None, 1, tn), lambda l, j: (l, 0, j)),
        out_shape=jax.ShapeDtypeStruct((depth, 1, n), F32),
        compiler_params=_params("parallel", "parallel"),
        name="adaln",
    )(c.reshape(d, 1), ada_w, ada_b.reshape(depth, 1, n))
    return out.reshape(depth, n)


NORM_ROW_CHUNK = 128


def _norm_matmul_kernel(x_ref, a_ref, b_ref, w_ref, ws_ref, o_ref, os_ref, h_ref):
    @pl.when(pl.program_id(1) == 0)
    def _():
        rows = min(NORM_ROW_CHUNK, x_ref.shape[0])

        def chunk(c, carry):
            sl = pl.ds(pl.multiple_of(c * rows, rows), rows)
            h_ref[sl, :] = (_rms(x_ref[sl, :], a_ref[...]) + b_ref[...]).astype(BF16)
            return carry

        lax.fori_loop(0, x_ref.shape[0] // rows, chunk, 0)
        os_ref[...] = jnp.dot(h_ref[...], ws_ref[...], preferred_element_type=F32)

    o_ref[...] = jnp.dot(h_ref[...], w_ref[...], preferred_element_type=F32).astype(o_ref.dtype)


def norm_matmul(x, a, b, w, w_side, *, tm=1024, tn=512, out_dtype=F32):
    t, d = x.shape
    n = w.shape[1]
    ns = w_side.shape[1]
    tm, tn = min(tm, t), min(tn, n)
    return pl.pallas_call(
        _norm_matmul_kernel,
        grid=(t // tm, n // tn),
        in_specs=[pl.BlockSpec((tm, d), lambda i, j: (i, 0), pipeline_mode=pl.Buffered(1)),
                  pl.BlockSpec((1, d), lambda i, j: (0, 0)),
                  pl.BlockSpec((1, d), lambda i, j: (0, 0)),
                  pl.BlockSpec((d, tn), lambda i, j: (0, j)),
                  pl.BlockSpec((d, ns), lambda i, j: (0, 0))],
        out_specs=[pl.BlockSpec((tm, tn), lambda i, j: (i, j)),
                   pl.BlockSpec((tm, ns), lambda i, j: (i, 0))],
        out_shape=[jax.ShapeDtypeStruct((t, n), out_dtype), jax.ShapeDtypeStruct((t, ns), F32)],
        scratch_shapes=[pltpu.VMEM((tm, d), BF16)],
        compiler_params=_params("parallel", "arbitrary"),
        name="norm_matmul",
    )(x, a, b, w, w_side)


def _matmul_residual_kernel(a_ref, w_ref, x_ref, g_ref, o_ref):
    y = jnp.dot(a_ref[...], w_ref[...], preferred_element_type=F32)
    o_ref[...] = x_ref[...] + g_ref[...] * y


def matmul_residual(a, w, x, g, *, tm=1024, tn=512):
    t, k = a.shape
    d = w.shape[1]
    tm, tn = min(tm, t), min(tn, d)
    return pl.pallas_call(
        _matmul_residual_kernel,
        grid=(t // tm, d // tn),
        in_specs=[pl.BlockSpec((tm, k), lambda i, j: (i, 0)),
                  pl.BlockSpec((k, tn), lambda i, j: (0, j)),
                  pl.BlockSpec((tm, tn), lambda i, j: (i, j)),
                  pl.BlockSpec((1, tn), lambda i, j: (0, j))],
        out_specs=pl.BlockSpec((tm, tn), lambda i, j: (i, j)),
        out_shape=jax.ShapeDtypeStruct((t, d), F32),
        compiler_params=_params("parallel", "parallel"),
        name="matmul_residual",
    )(a, w, x, g)


COLPREP_HEADS = 4


def _colprep_kernel(z_ref, g_ref, *o_refs, norm, scale, transpose, mean_rows):
    hd = HEAD_DIM
    for h in range(z_ref.shape[1] // hd):
        cols = slice(h * hd, (h + 1) * hd)
        y = z_ref[:, cols]
        if norm:
            y = _rms(y, g_ref[...])
        if mean_rows:
            tm = y.shape[0]
            o_refs[1][:, cols] = jnp.mean(y.reshape(tm // mean_rows, mean_rows, hd), axis=1)
        if scale != 1.0:
            y = y * scale
        if transpose:
            o_refs[0][cols, :] = y.T.astype(o_refs[0].dtype)
        else:
            o_refs[0][:, cols] = y.astype(o_refs[0].dtype)


def colprep(z, col0, nheads, gain=None, *, scale=1.0, transpose=False, mean_rows=0, tm=2048,
            out_dtype=BF16):
    t = z.shape[0]
    tm = min(tm, t)
    hb = COLPREP_HEADS
    bw = hb * HEAD_DIM
    assert nheads % hb == 0 and col0 % bw == 0
    c0 = col0 // bw
    norm = gain is not None
    g = (gain if norm else jnp.ones((HEAD_DIM,), F32)).reshape(1, HEAD_DIM)
    w = nheads * HEAD_DIM
    if transpose:
        out_shape = [jax.ShapeDtypeStruct((w, t), out_dtype)]
        out_specs = [pl.BlockSpec((bw, tm), lambda i, h: (h, i))]
    else:
        out_shape = [jax.ShapeDtypeStruct((t, w), out_dtype)]
        out_specs = [pl.BlockSpec((tm, bw), lambda i, h: (i, h))]
    if mean_rows:
        out_shape.append(jax.ShapeDtypeStruct((t // mean_rows, w), F32))
        out_specs.append(pl.BlockSpec((tm // mean_rows, bw), lambda i, h: (i, h)))
    outs = pl.pallas_call(
        functools.partial(_colprep_kernel, norm=norm, scale=scale, transpose=transpose,
                          mean_rows=mean_rows),
        grid=(t // tm, nheads // hb),
        in_specs=[pl.BlockSpec((tm, bw), lambda i, h: (i, c0 + h)),
                  pl.BlockSpec((1, HEAD_DIM), lambda i, h: (0, 0))],
        out_specs=out_specs,
        out_shape=out_shape,
        compiler_params=_params("parallel", "parallel"),
        name="colprep",
    )(z, g)
    return outs if mean_rows else outs[0]


ACC_ROWS = HEAD_DIM + 16


def _with_ones_rows(vt):
    return jnp.concatenate([vt, jnp.ones((ACC_ROWS - HEAD_DIM, vt.shape[1]), vt.dtype)], axis=0)


def _flash_step(s_chunks, vt1, m, acc_ref, slot):
    m_new = m
    for s in s_chunks:
        m_new = jnp.maximum(m_new, jnp.max(s, axis=0, keepdims=True))
    alpha = jnp.exp2(m - m_new)
    ps = [jnp.exp2((s - m_new).astype(BF16)) for s in s_chunks]
    p16 = jnp.concatenate(ps, axis=0) if len(ps) > 1 else ps[0]
    acc_ref[slot] = alpha * acc_ref[slot] + jnp.dot(vt1, p16, preferred_element_type=F32)
    return m_new


def _moba_kernel(zq_ref, g_ref, kn_ref, vt_ref, km_ref, o_ref, bias_ref, acc_ref, *, tq, nb, topk):
    i = pl.program_id(1)
    qn = _rms(zq_ref[...], g_ref[...])
    qs = (qn * (HEAD_DIM ** -0.5 * LOG2E)).astype(BF16)
    pos = i * tq + lax.broadcasted_iota(I32, (1, tq), 1)
    cur = (pos // MOBA_BLOCK).astype(F32)
    bidx = lax.broadcasted_iota(I32, (nb, tq), 0).astype(F32)

    gate = jnp.where(bidx < cur, _dot_nt_f32(km_ref[...], qn), -jnp.inf)
    sel = jnp.zeros((nb, tq), F32)
    for _ in range(topk):
        mx = jnp.max(gate, axis=0, keepdims=True)
        is_max = (gate == mx) & (gate > -jnp.inf)
        first = jnp.min(jnp.where(is_max, bidx, float(nb)), axis=0, keepdims=True)
        pick = bidx == first
        sel = jnp.where(pick, 1.0, sel)
        gate = jnp.where(pick, -jnp.inf, gate)
    sel = jnp.where(bidx == cur, 1.0, sel)
    bias_ref[...] = jnp.where(sel > 0.5, 0.0, NEG)
    acc_ref[...] = jnp.zeros_like(acc_ref)
    nblk = tq // MOBA_BLOCK

    def scores(j, causal):
        off = pl.multiple_of(j * tq, tq)
        chunks = []
        for b in range(nblk):
            k = kn_ref[pl.ds(off + b * MOBA_BLOCK, MOBA_BLOCK), :]
            s = (lax.dot_general(k, qs, _NT, preferred_element_type=F32)
                 + bias_ref[pl.ds(j * nblk + b, 1), :])
            if causal:
                kpos = off + b * MOBA_BLOCK + lax.broadcasted_iota(I32, (MOBA_BLOCK, 1), 0)
                s = jnp.where(kpos <= pos, s, NEG)
            chunks.append(s)
        return chunks

    def update(j, chunks, m):
        off = pl.multiple_of(j * tq, tq)
        return _flash_step(chunks, _with_ones_rows(vt_ref[:, pl.ds(off, tq)]), m, acc_ref, 0)

    def pair(jj, m):
        j = odd + 2 * jj
        s0, s1 = scores(j, False), scores(j + 1, False)
        return update(j + 1, s1, update(j, s0, m))

    odd = i % 2
    m = jnp.full((1, tq), NEG, F32)
    m = lax.cond(odd == 1, lambda mm: update(0, scores(0, False), mm), lambda mm: mm, m)
    m = lax.fori_loop(0, i // 2, pair, m)
    update(i, scores(i, True), m)
    acc = acc_ref[0]
    o_ref[...] = (acc[:HEAD_DIM] / acc[HEAD_DIM:HEAD_DIM + 1]).T.astype(o_ref.dtype)


def moba_attention(z, q_col0, q_gain, kn, vt, kmean, *, tq=1024):
    t = z.shape[0]
    tq = min(tq, t)
    nb = t // MOBA_BLOCK
    h = MOBA_HEADS
    c0 = q_col0 // HEAD_DIM
    return pl.pallas_call(
        functools.partial(_moba_kernel, tq=tq, nb=nb, topk=min(MOBA_TOPK, nb)),
        grid=(h, t // tq),
        in_specs=[pl.BlockSpec((tq, HEAD_DIM), lambda hh, i: (i, c0 + hh)),
                  pl.BlockSpec((1, HEAD_DIM), lambda hh, i: (0, 0)),
                  pl.BlockSpec((t, HEAD_DIM), lambda hh, i: (0, hh)),
                  pl.BlockSpec((HEAD_DIM, t), lambda hh, i: (hh, 0)),
                  pl.BlockSpec((nb, HEAD_DIM), lambda hh, i: (0, hh))],
        out_specs=pl.BlockSpec((tq, HEAD_DIM), lambda hh, i: (i, hh)),
        out_shape=jax.ShapeDtypeStruct((t, h * HEAD_DIM), BF16),
        scratch_shapes=[pltpu.VMEM((nb, tq), F32), pltpu.VMEM((1, ACC_ROWS, tq), F32)],
        compiler_params=_params("parallel", "parallel"),
        name="moba",
    )(z, q_gain.reshape(1, HEAD_DIM), kn, vt, kmean)


def _mlstm_kernel(zq_ref, zk_ref, zv_ref, zo_ref, zg_ref, gb_ref, og_ref, o_ref, ct_ref, n_ref, m_ref,
                  *, chunk):
    dk, dv, cap = MLSTM_QK_DIM, MLSTM_V_DIM, MLSTM_GATE_CAP

    @pl.when(pl.program_id(0) == 0)
    def _():
        ct_ref[...] = jnp.zeros_like(ct_ref)
        n_ref[...] = jnp.zeros_like(n_ref)
        m_ref[...] = jnp.zeros_like(m_ref)

    pre = cap * jnp.tanh((zg_ref[...] + gb_ref[...]) / cap)
    log_f = jnp.minimum(pre, 0.0) - jnp.log1p(jnp.exp(-jnp.abs(pre)))
    row = lax.broadcasted_iota(I32, (chunk, chunk), 0)
    col = lax.broadcasted_iota(I32, (chunk, chunk), 1)
    causal = col <= row
    tri = causal.astype(BF16)
    p1 = log_f.astype(BF16)
    r1 = log_f - p1.astype(F32)
    p2 = r1.astype(BF16)
    p3 = (r1 - p2.astype(F32)).astype(BF16)
    mm = functools.partial(jnp.dot, preferred_element_type=F32)
    b = mm(tri, p1) + (mm(tri, p2) + mm(tri, p3))
    pre_t = pre.T
    b_t = b.T

    for h in range(MLSTM_HEADS):
        li_row = pre_t[h:h + 1, :]
        li_col = pre[:, h:h + 1]
        b_row = b_t[MLSTM_HEADS + h:MLSTM_HEADS + h + 1, :]
        b_col = b[:, MLSTM_HEADS + h:MLSTM_HEADS + h + 1]
        m_prev = m_ref[h][:, :1]
        log_d = jnp.where(causal, b_col - b_row + li_row, -jnp.inf)
        log_inter = b_col + m_prev
        m_s = jnp.maximum(log_inter, jnp.max(log_d, axis=1, keepdims=True))
        dmat = jnp.exp(log_d - m_s)
        inter_w = jnp.exp(log_inter - m_s)
        qf = zq_ref[:, h * dk:(h + 1) * dk] * dk ** -0.5
        kf = zk_ref[:, h * dk:(h + 1) * dk]
        q16 = qf.astype(BF16)
        v16 = zv_ref[:, h * dv:(h + 1) * dv].astype(BF16)
        s = lax.dot_general(q16, kf.astype(BF16), _NT, preferred_element_type=F32) * dmat
        ct = ct_ref[h]
        num = mm(s.astype(BF16), v16) + inter_w * mm(q16, ct.astype(BF16))
        den = (jnp.sum(s, axis=1, keepdims=True)
               + inter_w * jnp.sum(qf * n_ref[h], axis=1, keepdims=True))
        hh = num / jnp.maximum(jnp.abs(den), jnp.exp(-m_s))
        hn = _rms(hh, og_ref[:, h * dv:(h + 1) * dv])
        o_ref[:, h * dv:(h + 1) * dv] = (hn * jax.nn.sigmoid(zo_ref[:, h * dv:(h + 1) * dv])).astype(o_ref.dtype)
        b_end = b_col[chunk - 1:chunk, :]
        log_w = b_end - b_col + li_col
        m_new = jnp.maximum(b_end + m_prev, jnp.max(log_w, axis=0, keepdims=True))
        kw = kf * jnp.exp(log_w - m_new)
        decay = jnp.exp(b_end + m_prev - m_new)
        ct_ref[h] = decay * ct + lax.dot_general(kw.astype(BF16), v16, _TN, preferred_element_type=F32)
        n_ref[h] = decay * n_ref[h] + jnp.sum(kw, axis=0, keepdims=True)
        m_ref[h] = jnp.broadcast_to(m_new, (1, LANES))


def mlstm(z, q_col0, k_col0, v_col0, o_col0, zg, gate_bias, out_gain, *, chunk=256):
    t = z.shape[0]
    chunk = min(chunk, t)
    wk, wv = MLSTM_HEADS * MLSTM_QK_DIM, MLSTM_HEADS * MLSTM_V_DIM
    return pl.pallas_call(
        functools.partial(_mlstm_kernel, chunk=chunk),
        grid=(t // chunk,),
        in_specs=[pl.BlockSpec((chunk, wk), lambda c: (c, q_col0 // wk)),
                  pl.BlockSpec((chunk, wk), lambda c: (c, k_col0 // wk)),
                  pl.BlockSpec((chunk, wv), lambda c: (c, v_col0 // wv)),
                  pl.BlockSpec((chunk, wv), lambda c: (c, o_col0 // wv)),
                  pl.BlockSpec((chunk, LANES), lambda c: (c, 0)),
                  pl.BlockSpec((1, LANES), lambda c: (0, 0)),
                  pl.BlockSpec((1, wv), lambda c: (0, 0))],
        out_specs=pl.BlockSpec((chunk, wv), lambda c: (c, 0)),
        out_shape=jax.ShapeDtypeStruct((t, wv), BF16),
        scratch_shapes=[pltpu.VMEM((MLSTM_HEADS, MLSTM_QK_DIM, MLSTM_V_DIM), F32),
                        pltpu.VMEM((MLSTM_HEADS, 1, MLSTM_QK_DIM), F32),
                        pltpu.VMEM((MLSTM_HEADS, 1, LANES), F32)],
        compiler_params=_params("arbitrary"),
        name="mlstm",
    )(z, z, z, z, zg, gate_bias, out_gain.reshape(1, wv))


def _nsa_compress_kernel(a_ref, pe_ref, w1_ref, b1_ref, w2_ref, g_ref, o_ref, *, is_key):
    n, half = a_ref.shape
    w1 = w1_ref[...]
    bias = jnp.sum(w1 * pe_ref[...], axis=0, keepdims=True) + b1_ref[...]
    w1 = w1.astype(BF16)
    a16 = a_ref[...].astype(BF16)
    u = jnp.dot(a16, w1[:half], preferred_element_type=F32)
    v = jnp.dot(a16, w1[half:], preferred_element_type=F32)
    x = u + pltpu.roll(v, n - 1, 0) + bias
    hid = 0.5 * x * (1.0 + jnp.tanh(np.sqrt(2.0 / np.pi) * (x + 0.044715 * (x * x * x))))
    y = jnp.dot(hid.astype(BF16), w2_ref[...].astype(BF16), preferred_element_type=F32)
    if is_key:
        y = _rms(y, g_ref[...])
    o_ref[...] = y.astype(o_ref.dtype)


def nsa_compress(a, pe, w1, b1, w2, gain, *, is_key):
    g, n, half = a.shape
    hd = HEAD_DIM
    out_shape, out_spec = jax.ShapeDtypeStruct((g, n, hd), BF16), pl.BlockSpec((None, n, hd), lambda i: (i, 0, 0))
    full = lambda shape: pl.BlockSpec(shape, lambda i: (0,) * len(shape))
    return pl.pallas_call(
        functools.partial(_nsa_compress_kernel, is_key=is_key),
        grid=(g,),
        in_specs=[pl.BlockSpec((None, n, half), lambda i: (i, 0, 0)),
                  full((2 * half, 1)), full((2 * half, hd)), full((1, hd)), full((hd, hd)), full((1, hd))],
        out_specs=out_spec,
        out_shape=out_shape,
        compiler_params=_params("parallel"),
        name="nsa_compress",
    )(a, pe.reshape(2 * half, 1), w1, b1.reshape(1, hd), w2, gain.reshape(1, hd))


NSA_SLC_KEY_TILE = 1024
NSA_SLC_SUB_KEYS = 256


def _nsa_kernel(zq_ref, qg_ref, gate_ref, kc_ref, vc_ref, mt_ref, ks_ref, vst_ref, kw_ref, vwt_ref,
                o_ref, bias_ref, acc_ref, *, tq, nc, ns, ntop, kt):
    i = pl.program_id(1)
    r_heads = NSA_HEADS // NSA_KV_HEADS
    w = r_heads * tq
    hd = HEAD_DIM
    scale = hd ** -0.5 * LOG2E
    qs = jnp.concatenate(
        [(_rms(zq_ref[:, r * hd:(r + 1) * hd], qg_ref[...]) * scale).astype(BF16) for r in range(r_heads)],
        axis=0)
    pos_t = i * tq + lax.broadcasted_iota(I32, (1, tq), 1)
    pos = i * tq + (lax.broadcasted_iota(I32, (1, w), 1) & (tq - 1))

    s = lax.dot_general(kc_ref[...], qs, _NT, preferred_element_type=F32)
    cmp_end = lax.broadcasted_iota(I32, (nc, 1), 0) * NSA_CMP_STRIDE + (NSA_CMP_BLOCK - 1)
    cmask = cmp_end <= pos
    s = jnp.where(cmask, s, NEG)
    p = jnp.where(cmask, jnp.exp2(s - jnp.max(s, axis=0, keepdims=True)), 0.0)
    den = jnp.sum(p, axis=0, keepdims=True)
    p = p * (1.0 / jnp.where(den > 0.0, den, 1.0))
    o_cmp = lax.dot_general(vc_ref[...], p.astype(BF16), _TN, preferred_element_type=F32)
    imp = p[:, 0:tq]
    for r in range(1, r_heads):
        imp = imp + p[:, r * tq:(r + 1) * tq]
    imp_hi, imp_lo = _split_bf16(imp)
    p_slc = (jnp.dot(mt_ref[...], imp_hi, preferred_element_type=F32)
             + jnp.dot(mt_ref[...], imp_lo, preferred_element_type=F32))

    blk = lax.broadcasted_iota(I32, (ns, tq), 0).astype(F32)
    cur = (pos_t // NSA_SLC_BLOCK).astype(F32)
    forced = (blk == 0.0) | (blk == cur) | (blk == cur - 1.0)
    score0 = jnp.where(blk > cur, -jnp.inf, jnp.where(forced, jnp.inf, p_slc))

    blk_l = blk[:, :LANES]

    def pick_one(_, carry):
        score, sel = carry
        mx = jnp.max(score, axis=0, keepdims=True)
        is_max = (score == mx) & (score > -jnp.inf)
        first = jnp.min(jnp.where(is_max, blk_l, float(ns)), axis=0, keepdims=True)
        pick = blk_l == first
        return jnp.where(pick, -jnp.inf, score), jnp.where(pick, 1.0, sel)

    sel = jnp.concatenate(
        [lax.fori_loop(0, ntop, pick_one,
                       (score0[:, c * LANES:(c + 1) * LANES], jnp.zeros((ns, LANES), F32)))[1]
         for c in range(tq // LANES)], axis=1)
    bias = jnp.where(sel > 0.5, 0.0, NEG)
    bias_ref[...] = jnp.concatenate([bias] * r_heads, axis=1)
    acc_ref[...] = jnp.zeros_like(acc_ref)

    sub = NSA_SLC_SUB_KEYS
    per_sub = sub // NSA_SLC_BLOCK

    def slc_scores(j, causal):
        off = pl.multiple_of(j * kt, kt)
        blk0 = j * (kt // NSA_SLC_BLOCK)
        chunks = []
        for c in range(kt // sub):
            s = lax.dot_general(ks_ref[pl.ds(off + c * sub, sub), :], qs, _NT, preferred_element_type=F32)
            for b in range(per_sub):
                r = c * per_sub + b
                sb = s[b * NSA_SLC_BLOCK:(b + 1) * NSA_SLC_BLOCK, :] + bias_ref[pl.ds(blk0 + r, 1), :]
                if causal:
                    kpos = off + r * NSA_SLC_BLOCK + lax.broadcasted_iota(I32, (NSA_SLC_BLOCK, 1), 0)
                    sb = jnp.where(kpos <= pos, sb, NEG)
                chunks.append(sb)
        return chunks

    def slc_update(j, chunks, m):
        off = pl.multiple_of(j * kt, kt)
        return _flash_step(chunks, _with_ones_rows(vst_ref[:, pl.ds(off, kt)]), m, acc_ref, 0)

    def slc_pair(jj, m):
        j = odd + 2 * jj
        s0, s1 = slc_scores(j, False), slc_scores(j + 1, False)
        return slc_update(j + 1, s1, slc_update(j, s0, m))

    n_past = (i * tq) // kt
    odd = n_past % 2
    m = jnp.full((1, w), NEG, F32)
    m = lax.cond(odd == 1, lambda mm: slc_update(0, slc_scores(0, False), mm), lambda mm: mm, m)
    m = lax.fori_loop(0, n_past // 2, slc_pair, m)
    slc_update(n_past, slc_scores(n_past, True), m)

    wk = NSA_WINDOW + tq
    off = pl.multiple_of(jnp.maximum(i * tq - NSA_WINDOW, 0), tq)
    s = lax.dot_general(kw_ref[pl.ds(off, wk), :], qs, _NT, preferred_element_type=F32)
    kpos = off + lax.broadcasted_iota(I32, (wk, 1), 0)
    s = jnp.where((kpos <= pos) & (kpos > pos - NSA_WINDOW), s, NEG)
    p = jnp.exp2(s - jnp.max(s, axis=0, keepdims=True))
    o_win = (jnp.dot(vwt_ref[:, pl.ds(off, wk)], p.astype(BF16), preferred_element_type=F32)
             / jnp.sum(p, axis=0, keepdims=True))

    gates = jax.nn.sigmoid(gate_ref[...])
    acc = acc_ref[0]
    o_slc = acc[:hd] / acc[hd:hd + 1]
    for r in range(r_heads):
        cols = slice(r * tq, (r + 1) * tq)
        o = (gates[0, r:r + 1, :] * o_cmp[:, cols] + gates[1, r:r + 1, :] * o_slc[:, cols]
             + gates[2, r:r + 1, :] * o_win[:, cols])
        o_ref[:, r * hd:(r + 1) * hd] = o.T.astype(o_ref.dtype)


def nsa_slc_weights(ns, nc):
    ratio = NSA_SLC_BLOCK // NSA_CMP_STRIDE
    m = np.zeros((ns, nc), np.float32)
    for j in range(ns):
        for d, wgt in [(-1, 0.5)] + [(k, 1.0) for k in range(ratio - 1)] + [(ratio - 1, 0.5)]:
            n = ratio * j + d
            if 0 <= n < nc - 1:
                m[j, n] = wgt
    return m


def nsa_attention(z, q_col0, q_gain, gates_t, kcmp, vcmp, ksn, vs_t, kwn, vw_t, *, tq=256):
    t = z.shape[0]
    g, r_heads, hd = NSA_KV_HEADS, NSA_HEADS // NSA_KV_HEADS, HEAD_DIM
    nc, ns = t // NSA_CMP_STRIDE, t // NSA_SLC_BLOCK
    mt = jnp.asarray(nsa_slc_weights(ns, nc), BF16)
    qw = r_heads * hd
    once = pl.Buffered(1)
    return pl.pallas_call(
        functools.partial(_nsa_kernel, tq=tq, nc=nc, ns=ns, ntop=min(NSA_SLC_TOPN, ns),
                          kt=min(NSA_SLC_KEY_TILE, t)),
        grid=(g, t // tq),
        in_specs=[pl.BlockSpec((tq, qw), lambda gg, i: (i, q_col0 // qw + gg)),
                  pl.BlockSpec((1, hd), lambda gg, i: (0, 0)),
                  pl.BlockSpec((None, 3, r_heads, tq), lambda gg, i: (gg, 0, 0, i)),
                  pl.BlockSpec((None, nc, hd), lambda gg, i: (gg, 0, 0)),
                  pl.BlockSpec((None, nc, hd), lambda gg, i: (gg, 0, 0)),
                  pl.BlockSpec((ns, nc), lambda gg, i: (0, 0)),
                  pl.BlockSpec((t, hd), lambda gg, i: (0, gg), pipeline_mode=once),
                  pl.BlockSpec((hd, t), lambda gg, i: (gg, 0), pipeline_mode=once),
                  pl.BlockSpec((t, hd), lambda gg, i: (0, gg), pipeline_mode=once),
                  pl.BlockSpec((hd, t), lambda gg, i: (gg, 0), pipeline_mode=once)],
        out_specs=pl.BlockSpec((tq, qw), lambda gg, i: (i, gg)),
        out_shape=jax.ShapeDtypeStruct((t, NSA_HEADS * hd), BF16),
        scratch_shapes=[pltpu.VMEM((ns, r_heads * tq), F32), pltpu.VMEM((1, ACC_ROWS, r_heads * tq), F32)],
        compiler_params=_params("parallel", "arbitrary"),
        name="nsa",
    )(z, q_gain.reshape(1, hd), gates_t, kcmp, vcmp, mt, ksn, vs_t, kwn, vw_t)


def _dilated_kernel(q_ref, kp_ref, kc_ref, vp_ref, vc_ref, o_ref, lse_ref, *, tq, back):
    i = pl.program_id(1)
    hd = HEAD_DIM
    qpos = i * tq + lax.broadcasted_iota(I32, (tq, 1), 0)
    kpos = (i - 1) * tq + lax.broadcasted_iota(I32, (1, 2 * tq), 1)
    mask = (kpos <= qpos) & (kpos >= qpos - back) & (kpos >= 0)
    for j in range(DIL_HEADS):
        cols = slice(j * hd, (j + 1) * hd)
        k = jnp.concatenate([kp_ref[:, cols], kc_ref[:, cols]], axis=0)
        v = jnp.concatenate([vp_ref[:, cols], vc_ref[:, cols]], axis=0)
        s = lax.dot_general(q_ref[:, cols], k, _NT, preferred_element_type=F32)
        s = jnp.where(mask, s, NEG)
        m = jnp.max(s, axis=1, keepdims=True)
        p = jnp.exp(s - m)
        l = jnp.sum(p, axis=1, keepdims=True)
        o_ref[:, cols] = jnp.dot(p.astype(BF16), v, preferred_element_type=F32) / l
        lse_ref[:, cols] = jnp.broadcast_to(m + jnp.log(l), (tq, hd))


def dilated_group(qd, kd, vd, dil, *, back, tq=128):
    td, wd = qd.shape
    hw = wd // dil
    tq = min(tq, td)
    cur = pl.BlockSpec((tq, hw), lambda r, i: (i, r))
    prev = pl.BlockSpec((tq, hw), lambda r, i: (jnp.maximum(i - 1, 0), r))
    return pl.pallas_call(
        functools.partial(_dilated_kernel, tq=tq, back=back),
        grid=(dil, td // tq),
        in_specs=[cur, prev, cur, prev, cur],
        out_specs=[cur, cur],
        out_shape=[jax.ShapeDtypeStruct((td, wd), F32), jax.ShapeDtypeStruct((td, wd), F32)],
        compiler_params=_params("parallel", "parallel"),
        name="dilated",
    )(qd, kd, kd, vd, vd)


def _dilated_merge_kernel(*refs):
    n = (len(refs) - 1) // 2
    o_refs, lse_refs, out_ref = refs[:n], refs[n:2 * n], refs[2 * n]
    lse = [r[...] for r in lse_refs]
    m = functools.reduce(jnp.maximum, lse)
    e = [jnp.exp(x - m) for x in lse]
    tot = functools.reduce(lambda a, b: a + b, e)
    acc = sum(ei * r[...] for ei, r in zip(e, o_refs))
    out_ref[...] = (acc / tot).astype(out_ref.dtype)


def dilated_merge(outs, lses, *, tm=512):
    t, wd = outs[0].shape
    tm = min(tm, t)
    spec = pl.BlockSpec((tm, wd), lambda i: (i, 0))
    return pl.pallas_call(
        _dilated_merge_kernel,
        grid=(t // tm,),
        in_specs=[spec] * (2 * len(outs)),
        out_specs=spec,
        out_shape=jax.ShapeDtypeStruct((t, wd), BF16),
        compiler_params=_params("parallel"),
        name="dilated_merge",
    )(*outs, *lses)


def _top2_of_4(v):
    best, loc0 = v[0], jnp.zeros_like(v[0])
    for j in range(1, 4):
        better = v[j] > best
        best = jnp.where(better, v[j], best)
        loc0 = jnp.where(better, float(j), loc0)
    best1, loc1 = jnp.full_like(v[0], -jnp.inf), jnp.zeros_like(v[0])
    for j in range(4):
        better = (v[j] > best1) & (loc0 != float(j))
        best1 = jnp.where(better, v[j], best1)
        loc1 = jnp.where(better, float(j), loc1)
    return loc0, loc1


U32 = jnp.uint32


def _pack_bf16_halves(h):
    bits = lax.bitcast_convert_type(h.astype(BF16).astype(F32), U32)
    half = h.shape[1] // 2
    return bits[:, half:] | (bits[:, :half] >> 16)


def _unpack_bf16_halves(packed):
    lo = lax.bitcast_convert_type(packed << 16, F32).astype(BF16)
    hi = lax.bitcast_convert_type(packed & jnp.uint32(0xFFFF0000), F32).astype(BF16)
    return lo, hi


def _router_kernel(x_ref, a_ref, b_ref, wt_ref, rb_ref, h_ref, info_ref, cnt_ref, carry_ref, *, tm):
    ne, ng, eg = N_EXPERTS, N_EXPERT_GROUPS, EXPERTS_PER_GROUP

    @pl.when(pl.program_id(0) == 0)
    def _():
        carry_ref[...] = jnp.zeros_like(carry_ref)

    h = _rms(x_ref[...], a_ref[...]) + b_ref[...]
    h_ref[...] = _pack_bf16_halves(h)
    s = jax.nn.sigmoid(_dot_nt_f32(wt_ref[...], h))
    sel = s + rb_ref[...]
    srow = [s[e:e + 1, :] for e in range(ne)]
    selrow = [sel[e:e + 1, :] for e in range(ne)]
    gscore = []
    for g in range(ng):
        a0, a1, a2, a3 = selrow[eg * g:eg * g + 4]
        hi1, lo1, hi2, lo2 = jnp.maximum(a0, a1), jnp.minimum(a0, a1), jnp.maximum(a2, a3), jnp.minimum(a2, a3)
        gscore.append(jnp.maximum(hi1, hi2) + jnp.maximum(jnp.minimum(hi1, hi2), jnp.maximum(lo1, lo2)))
    best, grp = gscore[0], jnp.zeros_like(gscore[0])
    for g in range(1, ng):
        better = gscore[g] > best
        best = jnp.where(better, gscore[g], best)
        grp = jnp.where(better, float(g), grp)
    pick = lambda rows, j: sum(jnp.where(grp == float(g), rows[eg * g + j], 0.0) for g in range(ng))
    loc0, loc1 = _top2_of_4([pick(selrow, j) for j in range(eg)])
    s_in = [pick(srow, j) for j in range(eg)]
    s0 = sum(jnp.where(loc0 == float(j), s_in[j], 0.0) for j in range(eg))
    s1 = sum(jnp.where(loc1 == float(j), s_in[j], 0.0) for j in range(eg))
    e0 = grp * eg + loc0
    e1 = grp * eg + loc1
    tot = s0 + s1
    eidx = lax.broadcasted_iota(I32, (ne, tm), 0).astype(F32)
    oh0 = eidx == e0
    oh1 = eidx == e1
    onehot = (oh0 | oh1).astype(BF16)
    upper = (lax.broadcasted_iota(I32, (tm, tm), 0) < lax.broadcasted_iota(I32, (tm, tm), 1)).astype(BF16)
    before = jnp.dot(onehot, upper, preferred_element_type=F32) + carry_ref[:, :1]
    r0 = jnp.sum(jnp.where(oh0, before, 0.0), axis=0, keepdims=True)
    r1 = jnp.sum(jnp.where(oh1, before, 0.0), axis=0, keepdims=True)
    carry_ref[...] = carry_ref[...] + jnp.sum(onehot.astype(F32), axis=1, keepdims=True)
    cnt_ref[...] = carry_ref[...]
    rid = lax.broadcasted_iota(I32, (8, tm), 0)
    rows = (e0, e1, s0 / tot, s1 / tot, r0, r1)
    info = jnp.zeros((8, tm), F32)
    for k, r in enumerate(rows):
        info = jnp.where(rid == k, r, info)
    info_ref[...] = info


def moe_router(x, a, b, router_w, router_b, *, tm=512):
    t, d = x.shape
    tm = min(tm, t)
    ne = N_EXPERTS
    return pl.pallas_call(
        functools.partial(_router_kernel, tm=tm),
        grid=(t // tm,),
        in_specs=[pl.BlockSpec((tm, d), lambda i: (i, 0)),
                  pl.BlockSpec((1, d), lambda i: (0, 0)),
                  pl.BlockSpec((1, d), lambda i: (0, 0)),
                  pl.BlockSpec((ne, d), lambda i: (0, 0)),
                  pl.BlockSpec((ne, 1), lambda i: (0, 0))],
        out_specs=[pl.BlockSpec((tm, d // 2), lambda i: (i, 0)),
                   pl.BlockSpec((8, tm), lambda i: (0, i)),
                   pl.BlockSpec((ne, LANES), lambda i: (0, 0))],
        out_shape=[jax.ShapeDtypeStruct((t, d // 2), U32),
                   jax.ShapeDtypeStruct((8, t), F32),
                   jax.ShapeDtypeStruct((ne, LANES), F32)],
        scratch_shapes=[pltpu.VMEM((ne, LANES), F32)],
        compiler_params=_params("arbitrary"),
        name="moe_router",
    )(x, a, b, router_w.T, router_b.reshape(ne, 1))


def _scatter_rows_kernel(pos_ref, h_ref, init_ref, o_ref, sem, *, tm):
    del init_ref
    base = pl.program_id(0) * tm

    def row_copy(r, k):
        dst = pos_ref[2 * (base + r) + k]
        return pltpu.make_async_copy(h_ref.at[pl.ds(r, 1), :], o_ref.at[pl.ds(dst, 1), :], sem)

    def start(r, carry):
        row_copy(r, 0).start(priority=0)
        row_copy(r, 1).start(priority=1)
        return carry

    def wait(r, carry):
        row_copy(r, 0).wait()
        row_copy(r, 1).wait()
        return carry

    lax.fori_loop(0, tm, start, 0)
    lax.fori_loop(0, tm, wait, 0)


def scatter_rows(h, pos_flat, n_rows, *, tm=256):
    t, d = h.shape
    tm = min(tm, t)
    return pl.pallas_call(
        functools.partial(_scatter_rows_kernel, tm=tm),
        grid_spec=pltpu.PrefetchScalarGridSpec(
            num_scalar_prefetch=1,
            grid=(t // tm,),
            in_specs=[pl.BlockSpec((tm, d), lambda i, pos: (i, 0)),
                      pl.BlockSpec(memory_space=pl.ANY)],
            out_specs=pl.BlockSpec(memory_space=pl.ANY),
            scratch_shapes=[pltpu.SemaphoreType.DMA(())]),
        out_shape=jax.ShapeDtypeStruct((n_rows, d), h.dtype),
        input_output_aliases={2: 0},
        compiler_params=_params("arbitrary"),
        name="moe_scatter",
    )(pos_flat, h, jnp.zeros((n_rows, d), h.dtype))


def _expert_changed(te_ref, i):
    return (i == 0) | (te_ref[i] != te_ref[jnp.maximum(i - 1, 0)])


def _expert_up_kernel(te_ref, tv_ref, xs_ref, wg_ref, wu_ref, o_ref, wg16_ref, wu16_ref):
    i = pl.program_id(1)

    @pl.when(_expert_changed(te_ref, i))
    def _():
        wg16_ref[...] = wg_ref[...].astype(BF16)
        wu16_ref[...] = wu_ref[...].astype(BF16)

    @pl.when(tv_ref[i] > 0)
    def _():
        lo, hi = _unpack_bf16_halves(xs_ref[...])
        half = lo.shape[1]
        mm = functools.partial(jnp.dot, preferred_element_type=F32)
        g = mm(lo, wg16_ref[:half, :]) + mm(hi, wg16_ref[half:, :])
        u = mm(lo, wu16_ref[:half, :]) + mm(hi, wu16_ref[half:, :])
        o_ref[...] = (g * jax.nn.sigmoid(g) * u).astype(o_ref.dtype)

    @pl.when(tv_ref[i] == 0)
    def _():
        o_ref[...] = jnp.zeros_like(o_ref)


def expert_up(xs, w_gate, w_up, layer, tile_expert, tile_valid, *, tm, tf=512):
    p, dp = xs.shape
    d = w_gate.shape[2]
    f = w_gate.shape[3]
    tf = min(tf, f)
    w_spec = pl.BlockSpec((None, None, d, tf), lambda j, i, te, tv: (layer, te[i], 0, j),
                          pipeline_mode=pl.Buffered(1))
    return pl.pallas_call(
        _expert_up_kernel,
        grid_spec=pltpu.PrefetchScalarGridSpec(
            num_scalar_prefetch=2,
            grid=(f // tf, p // tm),
            in_specs=[pl.BlockSpec((tm, dp), lambda j, i, te, tv: (i, 0)), w_spec, w_spec],
            out_specs=pl.BlockSpec((tm, tf), lambda j, i, te, tv: (i, j)),
            scratch_shapes=[pltpu.VMEM((d, tf), BF16), pltpu.VMEM((d, tf), BF16)]),
        out_shape=jax.ShapeDtypeStruct((p, f), BF16),
        compiler_params=_params("arbitrary", "arbitrary"),
        name="moe_expert_up",
    )(tile_expert, tile_valid, xs, w_gate, w_up)


def _expert_down_kernel(te_ref, tv_ref, a_ref, wd_ref, o_ref, wd16_ref):
    i = pl.program_id(1)

    @pl.when(_expert_changed(te_ref, i))
    def _():
        wd16_ref[...] = wd_ref[...].astype(BF16)

    @pl.when(tv_ref[i] > 0)
    def _():
        o_ref[...] = jnp.dot(a_ref[...], wd16_ref[...], preferred_element_type=F32)

    @pl.when(tv_ref[i] == 0)
    def _():
        o_ref[...] = jnp.zeros_like(o_ref)


def expert_down(act, w_down, layer, tile_expert, tile_valid, *, tm, tn=2048):
    p, f = act.shape
    d = w_down.shape[3]
    tn = min(tn, d)
    return pl.pallas_call(
        _expert_down_kernel,
        grid_spec=pltpu.PrefetchScalarGridSpec(
            num_scalar_prefetch=2,
            grid=(d // tn, p // tm),
            in_specs=[pl.BlockSpec((tm, f), lambda j, i, te, tv: (i, 0)),
                      pl.BlockSpec((None, None, f, tn), lambda j, i, te, tv: (layer, te[i], 0, j))],
            out_specs=pl.BlockSpec((tm, tn), lambda j, i, te, tv: (i, j)),
            scratch_shapes=[pltpu.VMEM((f, tn), BF16)]),
        out_shape=jax.ShapeDtypeStruct((p, d), F32),
        compiler_params=_params("arbitrary", "arbitrary"),
        name="moe_expert_down",
    )(tile_expert, tile_valid, act, w_down)


def _combine_kernel(pos_ref, x_ref, g_ref, w_ref, y_ref, o_ref, buf_ref, sem, *, tm):
    base = pl.program_id(0) * tm

    def row_copy(r, k):
        src = pos_ref[2 * (base + r) + k]
        return pltpu.make_async_copy(y_ref.at[pl.ds(src, 1), :], buf_ref.at[k, pl.ds(r, 1), :], sem)

    def start(r, carry):
        row_copy(r, 0).start(priority=0)
        row_copy(r, 1).start(priority=1)
        return carry

    def wait(r, carry):
        row_copy(r, 0).wait()
        row_copy(r, 1).wait()
        return carry

    lax.fori_loop(0, tm, start, 0)
    lax.fori_loop(0, tm, wait, 0)
    w = w_ref[...]
    y = w[:, 0:1] * buf_ref[0] + w[:, 1:2] * buf_ref[1]
    o_ref[...] = x_ref[...] + g_ref[...] * y


def moe_combine(x, g, wts, y, pos_flat, *, tm=256):
    t, d = x.shape
    tm = min(tm, t)
    return pl.pallas_call(
        functools.partial(_combine_kernel, tm=tm),
        grid_spec=pltpu.PrefetchScalarGridSpec(
            num_scalar_prefetch=1,
            grid=(t // tm,),
            in_specs=[pl.BlockSpec((tm, d), lambda i, pos: (i, 0)),
                      pl.BlockSpec((1, d), lambda i, pos: (0, 0)),
                      pl.BlockSpec((tm, 2), lambda i, pos: (i, 0)),
                      pl.BlockSpec(memory_space=pl.ANY)],
            out_specs=pl.BlockSpec((tm, d), lambda i, pos: (i, 0)),
            scratch_shapes=[pltpu.VMEM((2, tm, d), F32), pltpu.SemaphoreType.DMA(())]),
        out_shape=jax.ShapeDtypeStruct((t, d), F32),
        compiler_params=_params("arbitrary"),
        name="moe_combine",
    )(pos_flat, x, g, wts, y)


MOE_ROW_TILE = 256


def moe_ffn_residual(x, a, b, g, router_w, router_b, w_gate, w_up, w_down, layer):
    t, d = x.shape
    ne, tm = N_EXPERTS, MOE_ROW_TILE
    h, info, cnt = moe_router(x, a, b, router_w, router_b)
    counts = cnt[:, 0].astype(I32)
    padded = (counts + tm - 1) // tm * tm
    ends = jnp.cumsum(padded)
    starts = ends - padded
    n_tiles = 2 * t // tm + ne
    tile_row = jnp.arange(n_tiles, dtype=I32) * tm
    tile_expert = jnp.minimum(jnp.sum((tile_row[:, None] >= ends[None, :]).astype(I32), axis=1), ne - 1)
    tile_valid = (tile_row < ends[-1]).astype(I32)
    experts = info[0:2].astype(I32)
    start_of = jnp.sum(jnp.where(experts[..., None] == jnp.arange(ne, dtype=I32), starts, 0), axis=-1)
    pos = (start_of + info[4:6].astype(I32)).T.reshape(-1)
    wts = info[2:4].T
    xs = scatter_rows(h, pos, n_tiles * tm)
    act = expert_up(xs, w_gate, w_up, layer, tile_expert, tile_valid, tm=tm)
    y = expert_down(act, w_down, layer, tile_expert, tile_valid, tm=tm)
    return moe_combine(x, g, wts, y, pos)


MOBA_W = MOBA_HEADS * HEAD_DIM
MLSTM_QK_W = MLSTM_HEADS * MLSTM_QK_DIM
MLSTM_V_W = MLSTM_HEADS * MLSTM_V_DIM
EVEN_MAIN_W = 3 * MOBA_W + 2 * MLSTM_QK_W + 2 * MLSTM_V_W
NSA_Q_W = NSA_HEADS * HEAD_DIM
NSA_KV_W = NSA_KV_HEADS * HEAD_DIM
NSA_GATE_W = 3 * NSA_HEADS
DIL_GROUP_W = DIL_HEADS * HEAD_DIM
DIL_W = len(DIL_PATTERNS) * DIL_GROUP_W
ODD_GATE_COL0 = NSA_Q_W + 6 * NSA_KV_W


def _pad_cols(w, width):
    return jnp.pad(w, ((0, 0), (0, width - w.shape[1])))


def moba_mlstm_mixer(x, a, b, g, w_in, w_out, qn_g, kn_g, i_b, f_b, out_g):
    w_main = w_in[:, :EVEN_MAIN_W].astype(BF16)
    w_gate = _pad_cols(w_in[:, EVEN_MAIN_W:], LANES).astype(BF16)
    z, zg = norm_matmul(x, a, b, w_main, w_gate)
    kn, kmean = colprep(z, MOBA_W, MOBA_HEADS, kn_g, mean_rows=MOBA_BLOCK)
    vt = colprep(z, 2 * MOBA_W, MOBA_HEADS, transpose=True)
    o_a = moba_attention(z, 0, qn_g, kn, vt, kmean)
    c_q = 3 * MOBA_W
    c_k = c_q + MLSTM_QK_W
    c_v = c_k + MLSTM_QK_W
    c_o = c_v + MLSTM_V_W
    gate_bias = _pad_cols(jnp.concatenate([i_b, f_b]).reshape(1, -1), LANES)
    o_b = mlstm(z, c_q, c_k, c_v, c_o, zg, gate_bias, out_g)
    o = jnp.concatenate([o_a, o_b], axis=1)
    return matmul_residual(o, w_out.astype(BF16), x, g)


def nsa_dilated_mixer(x, a, b, g, w_in, w_out, nsa_qn_g, nsa_kn_g, pe_k, phik_w1, phik_b1, phik_w2,
                      pe_v, phiv_w1, phiv_b1, phiv_w2, dil_qn_g, dil_kn_g):
    t = x.shape[0]
    hd, kvw = HEAD_DIM, NSA_KV_W
    gc = ODD_GATE_COL0
    w_main = jnp.concatenate([w_in[:, :gc], w_in[:, gc + NSA_GATE_W:]], axis=1).astype(BF16)
    w_gate = _pad_cols(w_in[:, gc:gc + NSA_GATE_W], LANES).astype(BF16)
    z, zg = norm_matmul(x, a, b, w_main, w_gate)
    r_heads = NSA_HEADS // NSA_KV_HEADS
    gates_t = zg[:, :NSA_GATE_W].reshape(t, NSA_KV_HEADS, r_heads, 3).transpose(1, 3, 2, 0)
    c_kc = NSA_Q_W

    def cmp_blocks(col0):
        s = NSA_CMP_STRIDE
        return (z[:, col0:col0 + kvw].reshape(t // s, s, NSA_KV_HEADS, hd)
                .transpose(2, 0, 1, 3).reshape(NSA_KV_HEADS, t // s, s * hd))

    kcmp = nsa_compress(cmp_blocks(c_kc), pe_k, phik_w1, phik_b1, phik_w2, nsa_kn_g, is_key=True)
    vcmp = nsa_compress(cmp_blocks(c_kc + kvw), pe_v, phiv_w1, phiv_b1, phiv_w2, nsa_kn_g, is_key=False)
    ksn = colprep(z, c_kc + 2 * kvw, NSA_KV_HEADS, nsa_kn_g)
    vs_t = colprep(z, c_kc + 3 * kvw, NSA_KV_HEADS, transpose=True)
    kwn = colprep(z, c_kc + 4 * kvw, NSA_KV_HEADS, nsa_kn_g)
    vw_t = colprep(z, c_kc + 5 * kvw, NSA_KV_HEADS, transpose=True)
    o_nsa = nsa_attention(z, 0, nsa_qn_g, gates_t, kcmp, vcmp, ksn, vs_t, kwn, vw_t)
    c_dq = gc
    n_dil = len(DIL_PATTERNS) * DIL_HEADS
    dqn = colprep(z, c_dq, n_dil, dil_qn_g, scale=hd ** -0.5)
    dkn = colprep(z, c_dq + DIL_W, n_dil, dil_kn_g)
    dvb = colprep(z, c_dq + 2 * DIL_W, n_dil)
    outs, lses = [], []
    for gi, (win, dil) in enumerate(DIL_PATTERNS):
        grp = lambda arr: arr[:, gi * DIL_GROUP_W:(gi + 1) * DIL_GROUP_W].reshape(t // dil, dil * DIL_GROUP_W)
        o_g, lse_g = dilated_group(grp(dqn), grp(dkn), grp(dvb), dil, back=win // dil)
        outs.append(o_g.reshape(t, DIL_GROUP_W))
        lses.append(lse_g.reshape(t, DIL_GROUP_W))
    o_dil = dilated_merge(outs, lses)
    o = jnp.concatenate([o_nsa, o_dil], axis=1)
    return matmul_residual(o, w_out.astype(BF16), x, g)


def kernel(x, c, ada_w, ada_b, norm_mix_g, norm_ffn_g, ev_w_in, ev_w_out, moba_qn_g, moba_kn_g, mlstm_i_b, mlstm_f_b, mlstm_out_g, od_w_in, od_w_out, nsa_qn_g, nsa_kn_g, nsa_pe_k, nsa_phik_w1, nsa_phik_b1, nsa_phik_w2, nsa_pe_v, nsa_phiv_w1, nsa_phiv_b1, nsa_phiv_w2, dil_qn_g, dil_kn_g, router_w, router_b, moe_w_gate, moe_w_up, moe_w_down):
    bsz, t, d = x.shape
    assert bsz == 1, "kernels are written for a single sequence"
    depth = ada_w.shape[0]
    mod = adaln(c, ada_w, ada_b)
    xs = x.reshape(t, d)
    for layer in range(depth):
        sh_m, sc_m, g_m, sh_f, sc_f, g_f = [m.reshape(1, d) for m in jnp.split(mod[layer], 6)]
        a_m = norm_mix_g[layer].reshape(1, d) * (1.0 + sc_m)
        a_f = norm_ffn_g[layer].reshape(1, d) * (1.0 + sc_f)
        j = layer // 2
        if layer % 2 == 0:
            xs = moba_mlstm_mixer(xs, a_m, sh_m, g_m, ev_w_in[j], ev_w_out[j], moba_qn_g[j], moba_kn_g[j],
                                  mlstm_i_b[j], mlstm_f_b[j], mlstm_out_g[j])
        else:
            xs = nsa_dilated_mixer(xs, a_m, sh_m, g_m, od_w_in[j], od_w_out[j], nsa_qn_g[j], nsa_kn_g[j],
                                   nsa_pe_k[j], nsa_phik_w1[j], nsa_phik_b1[j], nsa_phik_w2[j],
                                   nsa_pe_v[j], nsa_phiv_w1[j], nsa_phiv_b1[j], nsa_phiv_w2[j],
                                   dil_qn_g[j], dil_kn_g[j])
        xs = moe_ffn_residual(xs, a_f, sh_f, g_f, router_w, router_b, moe_w_gate, moe_w_up, moe_w_down, layer)
    return xs.reshape(bsz, t, d)
```
